```python
import math
import jax, jax.numpy as jnp
from jax import lax
import numpy as np


D_MODEL = 1024
BATCH = 8
SEQ = 4096
DEPTH = 2

SSM_GROUP = 16
N_GROUPS = D_MODEL // SSM_GROUP
SSM_STATE = 64
DT_MIN = 1e-3
DT_MAX = 1e-1
N_HEADS = 8
QK_NOPE = 128
QK_ROPE = 64
V_HEAD = 128
Q_LORA = 384
KV_LORA = 256
ROPE_THETA = 10000.0
Q_BLOCK = 128
SM_SCALE = (QK_NOPE + QK_ROPE) ** -0.5
NEG_INF = -1e30
D_FF = 4 * D_MODEL
N_A_LAYERS = DEPTH // 2
N_B_LAYERS = DEPTH - N_A_LAYERS
DN_ALPHA = (2 * DEPTH) ** 0.25
DN_BETA = (8 * DEPTH) ** -0.25
LN_EPS = 1e-5
RMS_EPS = 1e-6

kernel_name = 'yoco_s5_mla_sqrelu_deepnorm'


def layer_norm(x, g, b):
    xf = x.astype(jnp.float32)
    mu = jnp.mean(xf, axis=-1, keepdims=True)
    var = jnp.mean(jnp.square(xf - mu), axis=-1, keepdims=True)
    y = (xf - mu) * lax.rsqrt(var + LN_EPS) * g.astype(jnp.float32) + b.astype(jnp.float32)
    return y.astype(x.dtype)


def rms_norm(x, g):
    xf = x.astype(jnp.float32)
    y = xf * lax.rsqrt(jnp.mean(jnp.square(xf), axis=-1, keepdims=True) + RMS_EPS) * g.astype(jnp.float32)
    return y.astype(x.dtype)


def rope_tables(positions):
    half = QK_ROPE // 2
    inv_freq = ROPE_THETA ** (-jnp.arange(half, dtype=jnp.float32) / half)
    ang = positions.astype(jnp.float32)[..., None] * inv_freq
    return jnp.cos(ang), jnp.sin(ang)


def apply_rope(x, cos, sin):
    x1, x2 = jnp.split(x.astype(jnp.float32), 2, axis=-1)
    return jnp.concatenate([x1 * cos - x2 * sin, x1 * sin + x2 * cos], axis=-1).astype(x.dtype)


def _complex_linear_combine(left, right):
    ar_l, ai_l, hr_l, hi_l = left
    ar_r, ai_r, hr_r, hi_r = right
    return (ar_r * ar_l - ai_r * ai_l,
            ar_r * ai_l + ai_r * ar_l,
            ar_r * hr_l - ai_r * hi_l + hr_r,
            ar_r * hi_l + ai_r * hr_l + hi_r)


def s5_mixer(x, lam_re, lam_im, log_dt, b_re, b_im, c_re, c_im, d_skip, w_glu, w_out):
    f32 = jnp.float32
    bsz, seq, _ = x.shape
    u = x.astype(f32).reshape(bsz, seq, N_GROUPS, SSM_GROUP)
    lr = lam_re.astype(f32)
    li = lam_im.astype(f32)
    dt = jnp.exp(log_dt.astype(f32))[:, None]
    mag = jnp.exp(lr * dt)
    a_re = mag * jnp.cos(li * dt)
    a_im = mag * jnp.sin(li * dt)
    inv_den = 1.0 / (lr * lr + li * li)
    coef_re = ((a_re - 1.0) * lr + a_im * li) * inv_den
    coef_im = (a_im * lr - (a_re - 1.0) * li) * inv_den
    br = b_re.astype(f32)
    bi = b_im.astype(f32)
    bb_re = coef_re[..., None] * br - coef_im[..., None] * bi
    bb_im = coef_re[..., None] * bi + coef_im[..., None] * br
    bu_re = jnp.einsum('bsgc,gpc->bsgp', u, bb_re)
    bu_im = jnp.einsum('bsgc,gpc->bsgp', u, bb_im)
    shape_a = (1, seq, N_GROUPS, SSM_STATE)
    a_re_t = jnp.broadcast_to(a_re, shape_a)
    a_im_t = jnp.broadcast_to(a_im, shape_a)
    _, _, h_re, h_im = lax.associative_scan(
        _complex_linear_combine, (a_re_t, a_im_t, bu_re, bu_im), axis=1)
    y = (jnp.einsum('bsgp,gcp->bsgc', h_re, c_re.astype(f32))
         - jnp.einsum('bsgp,gcp->bsgc', h_im, c_im.astype(f32)))
    y = y + d_skip.astype(f32).reshape(N_GROUPS, SSM_GROUP) * u
    y = jax.nn.gelu(y.reshape(bsz, seq, D_MODEL)).astype(x.dtype)
    val, gate = jnp.split(y @ w_glu, 2, axis=-1)
    return (val * jax.nn.sigmoid(gate)) @ w_out


def mla_shared_kv(h, kv_w_a, kv_norm_g, kv_w_b, cos, sin):
    bsz, seq, _ = h.shape
    c_kv, k_rope = jnp.split(h @ kv_w_a, [KV_LORA], axis=-1)
    c_kv = rms_norm(c_kv, kv_norm_g)
    k_rope = apply_rope(k_rope, cos, sin)
    kv = (c_kv @ kv_w_b).reshape(bsz, seq, N_HEADS, QK_NOPE + V_HEAD)
    k_nope, v = jnp.split(kv, [QK_NOPE], axis=-1)
    return k_nope, k_rope, v


def mla_mixer(h, q_w_a, q_norm_g, q_w_b, w_o, k_nope, k_rope, v, cos, sin):
    bsz, seq, _ = h.shape
    c_q = rms_norm(h @ q_w_a, q_norm_g)
    q = (c_q @ q_w_b).reshape(bsz, seq, N_HEADS, QK_NOPE + QK_ROPE)
    q_nope, q_rope = jnp.split(q, [QK_NOPE], axis=-1)
    q_rope = apply_rope(q_rope, cos[:, :, None, :], sin[:, :, None, :])
    n_blocks = seq // Q_BLOCK

    def to_blocks(t):
        return t.reshape(bsz, n_blocks, Q_BLOCK, *t.shape[2:]).swapaxes(0, 1)

    key_pos = jnp.arange(seq)

    def attend_block(args):
        blk, qn, qr = args
        s = (jnp.einsum('bqhd,bkhd->bhqk', qn, k_nope, preferred_element_type=jnp.float32)
             + jnp.einsum('bqhr,bkr->bhqk', qr, k_rope, preferred_element_type=jnp.float32)) * SM_SCALE
        q_pos = blk * Q_BLOCK + jnp.arange(Q_BLOCK)
        s = jnp.where(key_pos[None, :] <= q_pos[:, None], s, NEG_INF)
        p = jax.nn.softmax(s, axis=-1).astype(v.dtype)
        return jnp.einsum('bhqk,bkhd->bqhd', p, v)

    o = lax.map(attend_block, (jnp.arange(n_blocks), to_blocks(q_nope), to_blocks(q_rope)))
    o = o.swapaxes(0, 1).reshape(bsz, seq, N_HEADS * V_HEAD)
    return o @ w_o


def sq_relu_mlp(h, w1, w2):
    return jnp.square(jax.nn.relu(h @ w1)) @ w2


def setup_inputs(seed: int = 0) -> dict:
    key = jax.random.key(seed)
    k = jax.random.split(key, 24)
    f32 = jnp.float32

    def nrm(i, shape, scale):
        return jax.random.normal(k[i], shape, f32) * scale

    n_idx = jnp.arange(SSM_STATE, dtype=f32)
    return {
        'x': nrm(0, (BATCH, SEQ, D_MODEL), 1.0),
        'positions': jnp.broadcast_to(jnp.arange(SEQ, dtype=jnp.int32), (BATCH, SEQ)),
        'ln_mix_g': 1.0 + nrm(1, (DEPTH, D_MODEL), 0.02),
        'ln_mix_b': nrm(2, (DEPTH, D_MODEL), 0.02),
        'ln_ffn_g': 1.0 + nrm(3, (DEPTH, D_MODEL), 0.02),
        'ln_ffn_b': nrm(4, (DEPTH, D_MODEL), 0.02),
        'w_ff1': nrm(5, (DEPTH, D_MODEL, D_FF), D_MODEL ** -0.5),
        'w_ff2': nrm(6, (DEPTH, D_FF, D_MODEL), D_FF ** -0.5 * DN_BETA),
        'ssm_lam_re': -0.5 + nrm(7, (N_A_LAYERS, N_GROUPS, SSM_STATE), 0.01),
        'ssm_lam_im': math.pi * n_idx + nrm(8, (N_A_LAYERS, N_GROUPS, SSM_STATE), 0.01),
        'ssm_log_dt': jax.random.uniform(k[9], (N_A_LAYERS, N_GROUPS), f32, math.log(DT_MIN), math.log(DT_MAX)),
        'ssm_b_re': nrm(10, (N_A_LAYERS, N_GROUPS, SSM_STATE, SSM_GROUP), (2 * SSM_GROUP) ** -0.5),
        'ssm_b_im': nrm(11, (N_A_LAYERS, N_GROUPS, SSM_STATE, SSM_GROUP), (2 * SSM_GROUP) ** -0.5),
        'ssm_c_re': nrm(12, (N_A_LAYERS, N_GROUPS, SSM_GROUP, SSM_STATE), SSM_STATE ** -0.5),
        'ssm_c_im': nrm(13, (N_A_LAYERS, N_GROUPS, SSM_GROUP, SSM_STATE), SSM_STATE ** -0.5),
        'ssm_d': nrm(14, (N_A_LAYERS, D_MODEL), 1.0),
        'ssm_w_glu': nrm(15, (N_A_LAYERS, D_MODEL, 2 * D_MODEL), D_MODEL ** -0.5),
        'ssm_w_out': nrm(16, (N_A_LAYERS, D_MODEL, D_MODEL), D_MODEL ** -0.5 * DN_BETA),
        'kv_w_a': nrm(17, (D_MODEL, KV_LORA + QK_ROPE), D_MODEL ** -0.5),
        'kv_norm_g': 1.0 + nrm(18, (KV_LORA,), 0.02),
        'kv_w_b': nrm(19, (KV_LORA, N_HEADS * (QK_NOPE + V_HEAD)), KV_LORA ** -0.5),
        'q_w_a': nrm(20, (N_B_LAYERS, D_MODEL, Q_LORA), D_MODEL ** -0.5),
        'q_norm_g': 1.0 + nrm(21, (N_B_LAYERS, Q_LORA), 0.02),
        'q_w_b': nrm(22, (N_B_LAYERS, Q_LORA, N_HEADS * (QK_NOPE + QK_ROPE)), Q_LORA ** -0.5),
        'attn_w_o': nrm(23, (N_B_LAYERS, N_HEADS * V_HEAD, D_MODEL), (N_HEADS * V_HEAD) ** -0.5 * DN_BETA),
    }


def reference(x, positions, ln_mix_g, ln_mix_b, ln_ffn_g, ln_ffn_b, w_ff1, w_ff2,
              ssm_lam_re, ssm_lam_im, ssm_log_dt, ssm_b_re, ssm_b_im, ssm_c_re, ssm_c_im,
              ssm_d, ssm_w_glu, ssm_w_out, kv_w_a, kv_norm_g, kv_w_b,
              q_w_a, q_norm_g, q_w_b, attn_w_o):
    cos, sin = rope_tables(positions)
    h = x
    k_nope = k_rope = v = None
    for layer in range(DEPTH):
        if layer < N_A_LAYERS:
            i = layer
            mix = s5_mixer(h, ssm_lam_re[i], ssm_lam_im[i], ssm_log_dt[i], ssm_b_re[i], ssm_b_im[i],
                           ssm_c_re[i], ssm_c_im[i], ssm_d[i], ssm_w_glu[i], ssm_w_out[i])
        else:
            if layer == N_A_LAYERS:
                k_nope, k_rope, v = mla_shared_kv(h, kv_w_a, kv_norm_g, kv_w_b, cos, sin)
            j = layer - N_A_LAYERS
            mix = mla_mixer(h, q_w_a[j], q_norm_g[j], q_w_b[j], attn_w_o[j], k_nope, k_rope, v, cos, sin)
        h = layer_norm(DN_ALPHA * h + mix, ln_mix_g[layer], ln_mix_b[layer])
        h = layer_norm(DN_ALPHA * h + sq_relu_mlp(h, w_ff1[layer], w_ff2[layer]), ln_ffn_g[layer], ln_ffn_b[layer])
    return h
```

```python
import functools
import math

import jax
import jax.numpy as jnp
from jax import lax
from jax.experimental import pallas as pl
from jax.experimental.pallas import tpu as pltpu

F32 = jnp.float32
BF16 = jnp.bfloat16

D_MODEL = 1024
DEPTH = 2
SSM_GROUP = 16
N_GROUPS = D_MODEL // SSM_GROUP
SSM_STATE = 64
N_HEADS = 8
QK_NOPE = 128
QK_ROPE = 64
V_HEAD = 128
Q_LORA = 384
KV_LORA = 256
ROPE_THETA = 10000.0
SM_SCALE = (QK_NOPE + QK_ROPE) ** -0.5
NEG_INF = -1e30
D_FF = 4 * D_MODEL
N_A_LAYERS = DEPTH // 2
DN_ALPHA = (2 * DEPTH) ** 0.25
LN_EPS = 1e-5
RMS_EPS = 1e-6

LANES = 128
SUBLANES = 8
MXU_DIM = 256
VMEM_LIMIT_BYTES = 56 * 1024 * 1024

S5_GROUPS_PER_TILE = MXU_DIM // SSM_GROUP
S5_N_TILES = N_GROUPS // S5_GROUPS_PER_TILE
S5_TILE_COLS = 2 * S5_GROUPS_PER_TILE * SSM_STATE
S5_SLABS_PER_TILE = S5_TILE_COLS // LANES
S5_PAIRS_PER_TILE = S5_SLABS_PER_TILE // 2
S5_N_SLABS = S5_N_TILES * S5_SLABS_PER_TILE
S5_T = 64
S5_PITCH = S5_T + SUBLANES

ROW_TILE = 512
ATTN_TQ = 256
ATTN_TK = 256


def _layer_norm(x, g, b):
    mu = jnp.mean(x, axis=-1, keepdims=True)
    xc = x - mu
    var = jnp.mean(xc * xc, axis=-1, keepdims=True)
    return xc * lax.rsqrt(var + LN_EPS) * g + b


def _rms_norm(x, g):
    return x * lax.rsqrt(jnp.mean(x * x, axis=-1, keepdims=True) + RMS_EPS) * g


def _gelu_tanh(x):
    c = math.sqrt(2.0 / math.pi)
    return 0.5 * x * (1.0 + jnp.tanh(c * (x + 0.044715 * (x * x * x))))


def _const_spec(shape):
    nd = len(shape)
    return pl.BlockSpec(shape, lambda *_: (0,) * nd, pipeline_mode=pl.Buffered(1))


def _s5_kernel(x_ref, wb_ref, wc_ref, ar_ref, ai_ref, d_ref, o_ref, st_ref, hc_ref):
    nb, t_len, _ = x_ref.shape
    rows = nb * t_len

    @pl.when(pl.program_id(0) == 0)
    def _():
        hc_ref[...] = jnp.zeros_like(hc_ref)

    x = x_ref[...].reshape(rows, D_MODEL)
    xb = x.astype(BF16)

    for kt in range(S5_N_TILES):
        bu = jnp.dot(xb[:, kt * MXU_DIM:(kt + 1) * MXU_DIM], wb_ref[kt],
                     preferred_element_type=F32)
        for j in range(S5_SLABS_PER_TILE):
            for b in range(nb):
                st_ref[kt * S5_SLABS_PER_TILE + j, pl.ds(b * S5_PITCH, t_len), :] = (
                    bu[b * t_len:(b + 1) * t_len, j * LANES:(j + 1) * LANES])

    for kt in range(S5_N_TILES):
        base = kt * S5_SLABS_PER_TILE
        a_re = [ar_ref[kt * S5_PAIRS_PER_TILE + j] for j in range(S5_PAIRS_PER_TILE)]
        a_im = [ai_ref[kt * S5_PAIRS_PER_TILE + j] for j in range(S5_PAIRS_PER_TILE)]
        h0 = tuple(hc_ref[base + j] for j in range(S5_SLABS_PER_TILE))

        def step(t, h, base=base, a_re=a_re, a_im=a_im):
            new = [None] * S5_SLABS_PER_TILE
            for j in range(S5_PAIRS_PER_TILE):
                rows_t = pl.ds(t, nb, stride=S5_PITCH)
                bu_re = st_ref[base + j, rows_t, :]
                bu_im = st_ref[base + S5_PAIRS_PER_TILE + j, rows_t, :]
                h_re, h_im = h[j], h[S5_PAIRS_PER_TILE + j]
                n_re = a_re[j] * h_re - a_im[j] * h_im + bu_re
                n_im = a_re[j] * h_im + a_im[j] * h_re + bu_im
                st_ref[base + j, rows_t, :] = n_re
                st_ref[base + S5_PAIRS_PER_TILE + j, rows_t, :] = n_im
                new[j] = n_re
                new[S5_PAIRS_PER_TILE + j] = n_im
            return tuple(new)

        h_last = lax.fori_loop(0, t_len, step, h0)
        for j in range(S5_SLABS_PER_TILE):
            hc_ref[base + j] = h_last[j]

    for nt in range(S5_N_TILES):
        parts = []
        for j in range(S5_SLABS_PER_TILE):
            col = [st_ref[nt * S5_SLABS_PER_TILE + j, pl.ds(b * S5_PITCH, t_len), :]
                   for b in range(nb)]
            parts.append(jnp.concatenate(col, axis=0).astype(BF16))
        h_tile = jnp.concatenate(parts, axis=1)
        y = jnp.dot(h_tile, wc_ref[nt], preferred_element_type=F32)
        cols = slice(nt * MXU_DIM, (nt + 1) * MXU_DIM)
        y = y + d_ref[:, cols] * x[:, cols]
        y = _gelu_tanh(y).astype(o_ref.dtype)
        for b in range(nb):
            o_ref[b, :, cols] = y[b * t_len:(b + 1) * t_len, :]


def _s5_scan(x, wb, wc, a_re, a_im, d_skip, t_len=S5_T):
    nb, seq, _ = x.shape
    assert nb == SUBLANES and seq % t_len == 0 and t_len % SUBLANES == 0
    pitch_rows = nb * (t_len + SUBLANES)
    return pl.pallas_call(
        _s5_kernel,
        grid=(seq // t_len,),
        in_specs=[
            pl.BlockSpec((nb, t_len, D_MODEL), lambda i: (0, i, 0)),
            _const_spec(wb.shape),
            _const_spec(wc.shape),
            _const_spec(a_re.shape),
            _const_spec(a_im.shape),
            _const_spec(d_skip.shape),
        ],
        out_specs=pl.BlockSpec((nb, t_len, D_MODEL), lambda i: (0, i, 0)),
        out_shape=jax.ShapeDtypeStruct(x.shape, BF16),
        scratch_shapes=[
            pltpu.VMEM((S5_N_SLABS, pitch_rows, LANES), F32),
            pltpu.VMEM((S5_N_SLABS, SUBLANES, LANES), F32),
        ],
        compiler_params=pltpu.CompilerParams(
            dimension_semantics=("arbitrary",), vmem_limit_bytes=VMEM_LIMIT_BYTES),
        name="s5_scan",
    )(x, wb, wc, a_re, a_im, d_skip)


def _s5_params(lam_re, lam_im, log_dt, b_re, b_im, c_re, c_im):
    dt = jnp.exp(log_dt)[:, None]
    mag = jnp.exp(lam_re * dt)
    a_re = mag * jnp.cos(lam_im * dt)
    a_im = mag * jnp.sin(lam_im * dt)
    inv_den = 1.0 / (lam_re * lam_re + lam_im * lam_im)
    coef_re = ((a_re - 1.0) * lam_re + a_im * lam_im) * inv_den
    coef_im = (a_im * lam_re - (a_re - 1.0) * lam_im) * inv_den
    bb_re = coef_re[..., None] * b_re - coef_im[..., None] * b_im
    bb_im = coef_re[..., None] * b_im + coef_im[..., None] * b_re
    gpt = S5_GROUPS_PER_TILE
    eye = jnp.eye(gpt, dtype=F32)

    def in_block(bb):
        bb = bb.reshape(S5_N_TILES, gpt, SSM_STATE, SSM_GROUP)
        blk = jnp.einsum('kgpc,gh->kgchp', bb, eye)
        return blk.reshape(S5_N_TILES, gpt * SSM_GROUP, gpt * SSM_STATE)

    def out_block(cc):
        cc = cc.reshape(S5_N_TILES, gpt, SSM_GROUP, SSM_STATE)
        blk = jnp.einsum('kgcp,gh->kgphc', cc, eye)
        return blk.reshape(S5_N_TILES, gpt * SSM_STATE, gpt * SSM_GROUP)

    wb = jnp.concatenate([in_block(bb_re), in_block(bb_im)], axis=2).astype(BF16)
    wc = jnp.concatenate([out_block(c_re), out_block(-c_im)], axis=1).astype(BF16)
    n_pairs = S5_N_SLABS // 2

    def bcast(a):
        return jnp.broadcast_to(a.reshape(n_pairs, 1, LANES), (n_pairs, SUBLANES, LANES))

    return wb, wc, bcast(a_re), bcast(a_im)


def _glu_out_ln_kernel(y_ref, x_ref, wg_ref, wo_ref, g_ref, b_ref, o_ref):
    z = jnp.dot(y_ref[...], wg_ref[...], preferred_element_type=F32)
    val, gate = z[:, :D_MODEL], z[:, D_MODEL:]
    m = (val * jax.nn.sigmoid(gate)).astype(BF16)
    mix = jnp.dot(m, wo_ref[...], preferred_element_type=F32)
    o_ref[...] = _layer_norm(DN_ALPHA * x_ref[...] + mix, g_ref[...], b_ref[...])


def _glu_out_ln(y, x, w_glu, w_out, g, b, tm=ROW_TILE):
    m = x.shape[0]
    row = lambda i: (i, 0)
    return pl.pallas_call(
        _glu_out_ln_kernel,
        grid=(m // tm,),
        in_specs=[pl.BlockSpec((tm, D_MODEL), row), pl.BlockSpec((tm, D_MODEL), row),
                  _const_spec(w_glu.shape), _const_spec(w_out.shape),
                  _const_spec(g.shape), _const_spec(b.shape)],
        out_specs=pl.BlockSpec((tm, D_MODEL), row),
        out_shape=jax.ShapeDtypeStruct((m, D_MODEL), F32),
        compiler_params=pltpu.CompilerParams(
            dimension_semantics=("parallel",), vmem_limit_bytes=VMEM_LIMIT_BYTES),
        name="glu_out_ln",
    )(y, x, w_glu, w_out, g, b)


def _mlp_ln_kernel(h_ref, w1_ref, w2_ref, g_ref, b_ref, o_ref):
    h = h_ref[...]
    hb = h.astype(BF16)
    acc = DN_ALPHA * h
    for c in range(D_FF // D_MODEL):
        cols = slice(c * D_MODEL, (c + 1) * D_MODEL)
        u = jnp.maximum(jnp.dot(hb, w1_ref[:, cols], preferred_element_type=F32), 0.0)
        acc = acc + jnp.dot((u * u).astype(BF16), w2_ref[cols, :], preferred_element_type=F32)
    o_ref[...] = _layer_norm(acc, g_ref[...], b_ref[...])


def _mlp_ln(h, w1, w2, g, b, tm=ROW_TILE):
    m = h.shape[0]
    row = lambda i: (i, 0)
    return pl.pallas_call(
        _mlp_ln_kernel,
        grid=(m // tm,),
        in_specs=[pl.BlockSpec((tm, D_MODEL), row), _const_spec(w1.shape), _const_spec(w2.shape),
                  _const_spec(g.shape), _const_spec(b.shape)],
        out_specs=pl.BlockSpec((tm, D_MODEL), row),
        out_shape=jax.ShapeDtypeStruct((m, D_MODEL), F32),
        compiler_params=pltpu.CompilerParams(
            dimension_semantics=("parallel",), vmem_limit_bytes=VMEM_LIMIT_BYTES),
        name="mlp_ln",
    )(h, w1, w2, g, b)


def _linear_ln_kernel(a_ref, r_ref, w_ref, g_ref, b_ref, o_ref):
    mix = jnp.dot(a_ref[...], w_ref[...], preferred_element_type=F32)
    o_ref[...] = _layer_norm(DN_ALPHA * r_ref[...] + mix, g_ref[...], b_ref[...])


def _linear_ln(a, res, w, g, b, tm=ROW_TILE):
    m = res.shape[0]
    row = lambda i: (i, 0)
    return pl.pallas_call(
        _linear_ln_kernel,
        grid=(m // tm,),
        in_specs=[pl.BlockSpec((tm, a.shape[1]), row), pl.BlockSpec((tm, D_MODEL), row),
                  _const_spec(w.shape), _const_spec(g.shape), _const_spec(b.shape)],
        out_specs=pl.BlockSpec((tm, D_MODEL), row),
        out_shape=jax.ShapeDtypeStruct((m, D_MODEL), F32),
        compiler_params=pltpu.CompilerParams(
            dimension_semantics=("parallel",), vmem_limit_bytes=VMEM_LIMIT_BYTES),
        name="linear_ln",
    )(a, res, w, g, b)


def _rope_tile(t2, cos_t, sin_t):
    return t2 * cos_t + pltpu.roll(t2, LANES // 2, axis=1) * sin_t


def _mla_proj_kernel(h_ref, cos_ref, sin_ref, kvwa_ref, kvg_ref, kvwb_ref, qwa_ref, qg_ref, qwb_ref,
                     q_ref, kn_ref, kr_ref, v_ref):
    hb = h_ref[...].astype(BF16)
    cos_t = cos_ref[...]
    sin_t = sin_ref[...]
    ckv = jnp.dot(hb, kvwa_ref[...], preferred_element_type=F32)
    kr_ref[...] = _rope_tile(ckv[:, KV_LORA:], cos_t, sin_t).astype(kr_ref.dtype)
    c_kv = _rms_norm(ckv[:, :KV_LORA], kvg_ref[...]).astype(BF16)
    kv = jnp.dot(c_kv, kvwb_ref[...], preferred_element_type=F32)
    for hd in range(N_HEADS):
        kn_ref[hd] = kv[:, hd * QK_NOPE:(hd + 1) * QK_NOPE].astype(kn_ref.dtype)
        v_ref[hd] = kv[:, N_HEADS * QK_NOPE + hd * V_HEAD:
                       N_HEADS * QK_NOPE + (hd + 1) * V_HEAD].astype(v_ref.dtype)
    c_q = _rms_norm(jnp.dot(hb, qwa_ref[...], preferred_element_type=F32), qg_ref[...]).astype(BF16)
    q = jnp.dot(c_q, qwb_ref[...], preferred_element_type=F32) * SM_SCALE
    for hd in range(N_HEADS):
        base = hd * 2 * LANES
        q_ref[hd, :, :LANES] = q[:, base:base + LANES].astype(q_ref.dtype)
        q_ref[hd, :, LANES:] = _rope_tile(q[:, base + LANES:base + 2 * LANES],
                                          cos_t, sin_t).astype(q_ref.dtype)


def _mla_proj(h, cos_t, sin_t, kv_wa, kv_g, kv_wb, q_wa, q_g, q_wb, nb, seq, tm=ROW_TILE):
    nt = seq // tm
    row = lambda b, i: (b * nt + i, 0)
    head = lambda b, i: (b, 0, i, 0)
    return pl.pallas_call(
        _mla_proj_kernel,
        grid=(nb, nt),
        in_specs=[pl.BlockSpec((tm, D_MODEL), row), pl.BlockSpec((tm, LANES), row),
                  pl.BlockSpec((tm, LANES), row),
                  _const_spec(kv_wa.shape), _const_spec(kv_g.shape), _const_spec(kv_wb.shape),
                  _const_spec(q_wa.shape), _const_spec(q_g.shape), _const_spec(q_wb.shape)],
        out_specs=[pl.BlockSpec((None, N_HEADS, tm, 2 * LANES), head),
                   pl.BlockSpec((None, N_HEADS, tm, QK_NOPE), head),
                   pl.BlockSpec((None, tm, LANES), lambda b, i: (b, i, 0)),
                   pl.BlockSpec((None, N_HEADS, tm, V_HEAD), head)],
        out_shape=[jax.ShapeDtypeStruct((nb, N_HEADS, seq, 2 * LANES), BF16),
                   jax.ShapeDtypeStruct((nb, N_HEADS, seq, QK_NOPE), BF16),
                   jax.ShapeDtypeStruct((nb, seq, LANES), BF16),
                   jax.ShapeDtypeStruct((nb, N_HEADS, seq, V_HEAD), BF16)],
        compiler_params=pltpu.CompilerParams(
            dimension_semantics=("parallel", "parallel"), vmem_limit_bytes=VMEM_LIMIT_BYTES),
        name="mla_proj",
    )(h, cos_t, sin_t, kv_wa, kv_g, kv_wb, q_wa, q_g, q_wb)


def _rot_cols(w):
    half = QK_ROPE // 2
    return jnp.concatenate([-w[..., half:], w[..., :half]], axis=-1)


def _attn_kernel(q_ref, kn_ref, kr_ref, v_ref, o_ref):
    tq = q_ref.shape[0]
    tk = ATTN_TK
    qi = pl.program_id(2)
    q = q_ref[...]

    def block(j, carry, masked):
        m, l, acc = carry
        rows = pl.ds(pl.multiple_of(j * tk, tk), tk)
        k = jnp.concatenate([kn_ref[rows, :], kr_ref[rows, :]], axis=1)
        s = lax.dot_general(q, k, (((1,), (1,)), ((), ())), preferred_element_type=F32)
        if masked:
            q_pos = lax.broadcasted_iota(jnp.int32, s.shape, 0)
            k_pos = lax.broadcasted_iota(jnp.int32, s.shape, 1)
            s = jnp.where(k_pos <= q_pos, s, NEG_INF)
        m_new = jnp.maximum(m, jnp.max(s, axis=-1, keepdims=True))
        alpha = jnp.exp(m - m_new)
        p = jnp.exp(s - m_new)
        l = alpha * l + jnp.sum(p, axis=-1, keepdims=True)
        acc = alpha * acc + jnp.dot(p.astype(BF16), v_ref[rows, :], preferred_element_type=F32)
        return m_new, l, acc

    init = (jnp.full((tq, 1), NEG_INF, F32), jnp.zeros((tq, 1), F32),
            jnp.zeros((tq, V_HEAD), F32))
    carry = lax.fori_loop(0, qi, functools.partial(block, masked=False), init)
    _, l, acc = block(qi, carry, masked=True)
    o_ref[...] = (acc / l).astype(o_ref.dtype)


def _attention(q, kn, kr, v, tq=ATTN_TQ):
    nb, nh, seq, _ = q.shape
    assert tq == ATTN_TK
    return pl.pallas_call(
        _attn_kernel,
        grid=(nb, nh, seq // tq),
        in_specs=[pl.BlockSpec((None, None, tq, 2 * LANES), lambda b, h, i: (b, h, i, 0)),
                  pl.BlockSpec((None, None, seq, QK_NOPE), lambda b, h, i: (b, h, 0, 0)),
                  pl.BlockSpec((None, seq, LANES), lambda b, h, i: (b, 0, 0)),
                  pl.BlockSpec((None, None, seq, V_HEAD), lambda b, h, i: (b, h, 0, 0))],
        out_specs=pl.BlockSpec((None, tq, V_HEAD), lambda b, h, i: (b, i, h)),
        out_shape=jax.ShapeDtypeStruct((nb, seq, nh * V_HEAD), BF16),
        compiler_params=pltpu.CompilerParams(
            dimension_semantics=("parallel", "parallel", "arbitrary"),
            vmem_limit_bytes=VMEM_LIMIT_BYTES),
        name="attention",
    )(q, kn, kr, v)


def _rope_tables(positions):
    half = QK_ROPE // 2
    inv_freq = ROPE_THETA ** (-jnp.arange(half, dtype=F32) / half)
    ang = positions.astype(F32)[..., None] * inv_freq
    zeros = jnp.zeros(ang.shape[:-1] + (LANES - QK_ROPE,), F32)
    cos_t = jnp.concatenate([jnp.cos(ang), jnp.cos(ang), zeros], axis=-1)
    sin_t = jnp.concatenate([jnp.sin(ang), jnp.sin(ang), zeros], axis=-1)
    return cos_t, sin_t


def _row(v):
    return v.reshape(1, -1).astype(F32)


def kernel(x, positions, ln_mix_g, ln_mix_b, ln_ffn_g, ln_ffn_b, w_ff1, w_ff2, ssm_lam_re, ssm_lam_im, ssm_log_dt, ssm_b_re, ssm_b_im, ssm_c_re, ssm_c_im, ssm_d, ssm_w_glu, ssm_w_out, kv_w_a, kv_norm_g, kv_w_b, q_w_a, q_norm_g, q_w_b, attn_w_o):
    nb, seq, _ = x.shape
    m = nb * seq
    cos_t, sin_t = _rope_tables(positions)
    cos_t = cos_t.reshape(m, LANES)
    sin_t = sin_t.reshape(m, LANES)

    kv_wa = jnp.concatenate([kv_w_a, _rot_cols(kv_w_a[:, KV_LORA:])], axis=1).astype(BF16)
    kv_wb3 = kv_w_b.reshape(KV_LORA, N_HEADS, QK_NOPE + V_HEAD)
    kv_wb = jnp.concatenate([kv_wb3[:, :, :QK_NOPE].reshape(KV_LORA, -1),
                             kv_wb3[:, :, QK_NOPE:].reshape(KV_LORA, -1)], axis=1).astype(BF16)

    h = x.reshape(m, D_MODEL)
    kn = kr = v = None
    for layer in range(DEPTH):
        if layer < N_A_LAYERS:
            i = layer
            wb, wc, a_re, a_im = _s5_params(ssm_lam_re[i], ssm_lam_im[i], ssm_log_dt[i],
                                            ssm_b_re[i], ssm_b_im[i], ssm_c_re[i], ssm_c_im[i])
            y = _s5_scan(h.reshape(nb, seq, D_MODEL), wb, wc, a_re, a_im, _row(ssm_d[i]))
            h = _glu_out_ln(y.reshape(m, D_MODEL), h, ssm_w_glu[i].astype(BF16),
                            ssm_w_out[i].astype(BF16), _row(ln_mix_g[layer]), _row(ln_mix_b[layer]))
        else:
            j = layer - N_A_LAYERS
            qwb3 = q_w_b[j].reshape(Q_LORA, N_HEADS, QK_NOPE + QK_ROPE)
            q_wb = jnp.concatenate([qwb3, _rot_cols(qwb3[:, :, QK_NOPE:])], axis=2)
            q_wb = q_wb.reshape(Q_LORA, N_HEADS * 2 * LANES).astype(BF16)
            q, kn_j, kr_j, v_j = _mla_proj(h, cos_t, sin_t, kv_wa, _row(kv_norm_g), kv_wb,
                                           q_w_a[j].astype(BF16), _row(q_norm_g[j]), q_wb, nb, seq)
            if layer == N_A_LAYERS:
                kn, kr, v = kn_j, kr_j, v_j
            o = _attention(q, kn, kr, v)
            h = _linear_ln(o.reshape(m, N_HEADS * V_HEAD), h, attn_w_o[j].astype(BF16),
                           _row(ln_mix_g[layer]), _row(ln_mix_b[layer]))
        h = _mlp_ln(h, w_ff1[layer].astype(BF16), w_ff2[layer].astype(BF16),
                    _row(ln_ffn_g[layer]), _row(ln_ffn_b[layer]))
    return h.reshape(nb, seq, D_MODEL)
```

```python
import functools
import math

import jax
import jax.numpy as jnp
from jax import lax
from jax.experimental import pallas as pl
from jax.experimental.pallas import tpu as pltpu

F32 = jnp.float32
BF16 = jnp.bfloat16

D_MODEL = 1024
DEPTH = 2
SSM_GROUP = 16
N_GROUPS = D_MODEL // SSM_GROUP
SSM_STATE = 64
N_HEADS = 8
QK_NOPE = 128
QK_ROPE = 64
V_HEAD = 128
Q_LORA = 384
KV_LORA = 256
ROPE_THETA = 10000.0
SM_SCALE = (QK_NOPE + QK_ROPE) ** -0.5
NEG_INF = -1e30
D_FF = 4 * D_MODEL
N_A_LAYERS = DEPTH // 2
DN_ALPHA = (2 * DEPTH) ** 0.25
LN_EPS = 1e-5
RMS_EPS = 1e-6

LANES = 128
SUBLANES = 8
MXU_DIM = 256
VMEM_LIMIT_BYTES = 56 * 1024 * 1024

S5_GROUPS_PER_TILE = MXU_DIM // SSM_GROUP
S5_N_TILES = N_GROUPS // S5_GROUPS_PER_TILE
S5_TILE_COLS = 2 * S5_GROUPS_PER_TILE * SSM_STATE
S5_SLABS_PER_TILE = S5_TILE_COLS // LANES
S5_PAIRS_PER_TILE = S5_SLABS_PER_TILE // 2
S5_N_SLABS = S5_N_TILES * S5_SLABS_PER_TILE
S5_T = 64
S5_PITCH = S5_T + SUBLANES

ROW_TILE = 512
ATTN_TQ = 512
ATTN_TK = 512


def _layer_norm(x, g, b):
    mu = jnp.mean(x, axis=-1, keepdims=True)
    xc = x - mu
    var = jnp.mean(xc * xc, axis=-1, keepdims=True)
    return xc * lax.rsqrt(var + LN_EPS) * g + b


def _rms_norm(x, g):
    return x * lax.rsqrt(jnp.mean(x * x, axis=-1, keepdims=True) + RMS_EPS) * g


def _gelu_tanh(x):
    c = math.sqrt(2.0 / math.pi)
    return 0.5 * x * (1.0 + jnp.tanh(c * (x + 0.044715 * (x * x * x))))


def _const_spec(shape):
    nd = len(shape)
    return pl.BlockSpec(shape, lambda *_: (0,) * nd, pipeline_mode=pl.Buffered(1))


def _s5_kernel(x_ref, wb_ref, wc_ref, ar_ref, ai_ref, d_ref, o_ref, st_ref, hc_ref):
    nb, t_len, _ = x_ref.shape
    rows = nb * t_len

    @pl.when(pl.program_id(0) == 0)
    def _():
        hc_ref[...] = jnp.zeros_like(hc_ref)

    x = x_ref[...].reshape(rows, D_MODEL)
    xb = x.astype(BF16)

    for kt in range(S5_N_TILES):
        bu = jnp.dot(xb[:, kt * MXU_DIM:(kt + 1) * MXU_DIM], wb_ref[kt],
                     preferred_element_type=F32)
        for j in range(S5_SLABS_PER_TILE):
            for b in range(nb):
                st_ref[kt * S5_SLABS_PER_TILE + j, pl.ds(b * S5_PITCH, t_len), :] = (
                    bu[b * t_len:(b + 1) * t_len, j * LANES:(j + 1) * LANES])

    for kt in range(S5_N_TILES):
        base = kt * S5_SLABS_PER_TILE
        a_re = [ar_ref[kt * S5_PAIRS_PER_TILE + j] for j in range(S5_PAIRS_PER_TILE)]
        a_im = [ai_ref[kt * S5_PAIRS_PER_TILE + j] for j in range(S5_PAIRS_PER_TILE)]
        h0 = tuple(hc_ref[base + j] for j in range(S5_SLABS_PER_TILE))

        def step(t, h, base=base, a_re=a_re, a_im=a_im):
            new = [None] * S5_SLABS_PER_TILE
            for j in range(S5_PAIRS_PER_TILE):
                rows_t = pl.ds(t, nb, stride=S5_PITCH)
                bu_re = st_ref[base + j, rows_t, :]
                bu_im = st_ref[base + S5_PAIRS_PER_TILE + j, rows_t, :]
                h_re, h_im = h[j], h[S5_PAIRS_PER_TILE + j]
                n_re = a_re[j] * h_re - a_im[j] * h_im + bu_re
                n_im = a_re[j] * h_im + a_im[j] * h_re + bu_im
                st_ref[base + j, rows_t, :] = n_re
                st_ref[base + S5_PAIRS_PER_TILE + j, rows_t, :] = n_im
                new[j] = n_re
                new[S5_PAIRS_PER_TILE + j] = n_im
            return tuple(new)

        h_last = lax.fori_loop(0, t_len, step, h0)
        for j in range(S5_SLABS_PER_TILE):
            hc_ref[base + j] = h_last[j]

    for nt in range(S5_N_TILES):
        parts = []
        for j in range(S5_SLABS_PER_TILE):
            col = [st_ref[nt * S5_SLABS_PER_TILE + j, pl.ds(b * S5_PITCH, t_len), :]
                   for b in range(nb)]
            parts.append(jnp.concatenate(col, axis=0).astype(BF16))
        h_tile = jnp.concatenate(parts, axis=1)
        y = jnp.dot(h_tile, wc_ref[nt], preferred_element_type=F32)
        cols = slice(nt * MXU_DIM, (nt + 1) * MXU_DIM)
        y = y + d_ref[:, cols] * x[:, cols]
        y = _gelu_tanh(y).astype(o_ref.dtype)
        for b in range(nb):
            o_ref[b, :, cols] = y[b * t_len:(b + 1) * t_len, :]


def _s5_scan(x, wb, wc, a_re, a_im, d_skip, t_len=S5_T):
    nb, seq, _ = x.shape
    assert nb == SUBLANES and seq % t_len == 0 and t_len % SUBLANES == 0
    pitch_rows = nb * (t_len + SUBLANES)
    return pl.pallas_call(
        _s5_kernel,
        grid=(seq // t_len,),
        in_specs=[
            pl.BlockSpec((nb, t_len, D_MODEL), lambda i: (0, i, 0)),
            _const_spec(wb.shape),
            _const_spec(wc.shape),
            _const_spec(a_re.shape),
            _const_spec(a_im.shape),
            _const_spec(d_skip.shape),
        ],
        out_specs=pl.BlockSpec((nb, t_len, D_MODEL), lambda i: (0, i, 0)),
        out_shape=jax.ShapeDtypeStruct(x.shape, BF16),
        scratch_shapes=[
            pltpu.VMEM((S5_N_SLABS, pitch_rows, LANES), F32),
            pltpu.VMEM((S5_N_SLABS, SUBLANES, LANES), F32),
        ],
        compiler_params=pltpu.CompilerParams(
            dimension_semantics=("arbitrary",), vmem_limit_bytes=VMEM_LIMIT_BYTES),
        name="s5_scan",
    )(x, wb, wc, a_re, a_im, d_skip)


def _s5_params(lam_re, lam_im, log_dt, b_re, b_im, c_re, c_im):
    dt = jnp.exp(log_dt)[:, None]
    mag = jnp.exp(lam_re * dt)
    a_re = mag * jnp.cos(lam_im * dt)
    a_im = mag * jnp.sin(lam_im * dt)
    inv_den = 1.0 / (lam_re * lam_re + lam_im * lam_im)
    coef_re = ((a_re - 1.0) * lam_re + a_im * lam_im) * inv_den
    coef_im = (a_im * lam_re - (a_re - 1.0) * lam_im) * inv_den
    bb_re = coef_re[..., None] * b_re - coef_im[..., None] * b_im
    bb_im = coef_re[..., None] * b_im + coef_im[..., None] * b_re
    gpt = S5_GROUPS_PER_TILE
    eye = jnp.eye(gpt, dtype=F32)

    def in_block(bb):
        bb = bb.reshape(S5_N_TILES, gpt, SSM_STATE, SSM_GROUP)
        blk = jnp.einsum('kgpc,gh->kgchp', bb, eye)
        return blk.reshape(S5_N_TILES, gpt * SSM_GROUP, gpt * SSM_STATE)

    def out_block(cc):
        cc = cc.reshape(S5_N_TILES, gpt, SSM_GROUP, SSM_STATE)
        blk = jnp.einsum('kgcp,gh->kgphc', cc, eye)
        return blk.reshape(S5_N_TILES, gpt * SSM_STATE, gpt * SSM_GROUP)

    wb = jnp.concatenate([in_block(bb_re), in_block(bb_im)], axis=2).astype(BF16)
    wc = jnp.concatenate([out_block(c_re), out_block(-c_im)], axis=1).astype(BF16)
    n_pairs = S5_N_SLABS // 2

    def bcast(a):
        return jnp.broadcast_to(a.reshape(n_pairs, 1, LANES), (n_pairs, SUBLANES, LANES))

    return wb, wc, bcast(a_re), bcast(a_im)


def _glu_out_ln_kernel(y_ref, x_ref, wg_ref, wo_ref, g_ref, b_ref, o_ref):
    z = jnp.dot(y_ref[...], wg_ref[...], preferred_element_type=F32)
    val, gate = z[:, :D_MODEL], z[:, D_MODEL:]
    m = (val * jax.nn.sigmoid(gate)).astype(BF16)
    mix = jnp.dot(m, wo_ref[...], preferred_element_type=F32)
    o_ref[...] = _layer_norm(DN_ALPHA * x_ref[...] + mix, g_ref[...], b_ref[...])


def _glu_out_ln(y, x, w_glu, w_out, g, b, tm=ROW_TILE):
    m = x.shape[0]
    row = lambda i: (i, 0)
    return pl.pallas_call(
        _glu_out_ln_kernel,
        grid=(m // tm,),
        in_specs=[pl.BlockSpec((tm, D_MODEL), row), pl.BlockSpec((tm, D_MODEL), row),
                  _const_spec(w_glu.shape), _const_spec(w_out.shape),
                  _const_spec(g.shape), _const_spec(b.shape)],
        out_specs=pl.BlockSpec((tm, D_MODEL), row),
        out_shape=jax.ShapeDtypeStruct((m, D_MODEL), F32),
        compiler_params=pltpu.CompilerParams(
            dimension_semantics=("parallel",), vmem_limit_bytes=VMEM_LIMIT_BYTES),
        name="glu_out_ln",
    )(y, x, w_glu, w_out, g, b)


def _mlp_ln_kernel(h_ref, w1_ref, w2_ref, g_ref, b_ref, o_ref):
    h = h_ref[...]
    hb = h.astype(BF16)
    acc = DN_ALPHA * h
    for c in range(D_FF // D_MODEL):
        cols = slice(c * D_MODEL, (c + 1) * D_MODEL)
        u = jnp.maximum(jnp.dot(hb, w1_ref[:, cols], preferred_element_type=F32), 0.0)
        acc = acc + jnp.dot((u * u).astype(BF16), w2_ref[cols, :], preferred_element_type=F32)
    o_ref[...] = _layer_norm(acc, g_ref[...], b_ref[...])


def _mlp_ln(h, w1, w2, g, b, tm=ROW_TILE):
    m = h.shape[0]
    row = lambda i: (i, 0)
    return pl.pallas_call(
        _mlp_ln_kernel,
        grid=(m // tm,),
        in_specs=[pl.BlockSpec((tm, D_MODEL), row), _const_spec(w1.shape), _const_spec(w2.shape),
                  _const_spec(g.shape), _const_spec(b.shape)],
        out_specs=pl.BlockSpec((tm, D_MODEL), row),
        out_shape=jax.ShapeDtypeStruct((m, D_MODEL), F32),
        compiler_params=pltpu.CompilerParams(
            dimension_semantics=("parallel",), vmem_limit_bytes=VMEM_LIMIT_BYTES),
        name="mlp_ln",
    )(h, w1, w2, g, b)


def _linear_ln_kernel(a_ref, r_ref, w_ref, g_ref, b_ref, o_ref):
    mix = jnp.dot(a_ref[...], w_ref[...], preferred_element_type=F32)
    o_ref[...] = _layer_norm(DN_ALPHA * r_ref[...] + mix, g_ref[...], b_ref[...])


def _linear_ln(a, res, w, g, b, tm=ROW_TILE):
    m = res.shape[0]
    row = lambda i: (i, 0)
    return pl.pallas_call(
        _linear_ln_kernel,
        grid=(m // tm,),
        in_specs=[pl.BlockSpec((tm, a.shape[1]), row), pl.BlockSpec((tm, D_MODEL), row),
                  _const_spec(w.shape), _const_spec(g.shape), _const_spec(b.shape)],
        out_specs=pl.BlockSpec((tm, D_MODEL), row),
        out_shape=jax.ShapeDtypeStruct((m, D_MODEL), F32),
        compiler_params=pltpu.CompilerParams(
            dimension_semantics=("parallel",), vmem_limit_bytes=VMEM_LIMIT_BYTES),
        name="linear_ln",
    )(a, res, w, g, b)


def _rope_tile(t2, cos_t, sin_t):
    return t2 * cos_t + pltpu.roll(t2, LANES // 2, axis=1) * sin_t


def _mla_proj_kernel(h_ref, cos_ref, sin_ref, kvwa_ref, kvg_ref, kvwb_ref, qwa_ref, qg_ref, qwb_ref,
                     q_ref, kn_ref, kr_ref, v_ref):
    hb = h_ref[...].astype(BF16)
    cos_t = cos_ref[...]
    sin_t = sin_ref[...]
    ckv = jnp.dot(hb, kvwa_ref[...], preferred_element_type=F32)
    kr_ref[...] = _rope_tile(ckv[:, KV_LORA:], cos_t, sin_t).astype(kr_ref.dtype)
    c_kv = _rms_norm(ckv[:, :KV_LORA], kvg_ref[...]).astype(BF16)
    kv = jnp.dot(c_kv, kvwb_ref[...], preferred_element_type=F32)
    for hd in range(N_HEADS):
        kn_ref[hd] = kv[:, hd * QK_NOPE:(hd + 1) * QK_NOPE].astype(kn_ref.dtype)
        v_ref[hd] = kv[:, N_HEADS * QK_NOPE + hd * V_HEAD:
                       N_HEADS * QK_NOPE + (hd + 1) * V_HEAD].astype(v_ref.dtype)
    c_q = _rms_norm(jnp.dot(hb, qwa_ref[...], preferred_element_type=F32), qg_ref[...]).astype(BF16)
    q = jnp.dot(c_q, qwb_ref[...], preferred_element_type=F32) * SM_SCALE
    for hd in range(N_HEADS):
        base = hd * 2 * LANES
        q_ref[hd, :, :LANES] = q[:, base:base + LANES].astype(q_ref.dtype)
        q_ref[hd, :, LANES:] = _rope_tile(q[:, base + LANES:base + 2 * LANES],
                                          cos_t, sin_t).astype(q_ref.dtype)


def _mla_proj(h, cos_t, sin_t, kv_wa, kv_g, kv_wb, q_wa, q_g, q_wb, nb, seq, tm=ROW_TILE):
    nt = seq // tm
    row = lambda b, i: (b * nt + i, 0)
    head = lambda b, i: (b, 0, i, 0)
    return pl.pallas_call(
        _mla_proj_kernel,
        grid=(nb, nt),
        in_specs=[pl.BlockSpec((tm, D_MODEL), row), pl.BlockSpec((tm, LANES), row),
                  pl.BlockSpec((tm, LANES), row),
                  _const_spec(kv_wa.shape), _const_spec(kv_g.shape), _const_spec(kv_wb.shape),
                  _const_spec(q_wa.shape), _const_spec(q_g.shape), _const_spec(q_wb.shape)],
        out_specs=[pl.BlockSpec((None, N_HEADS, tm, 2 * LANES), head),
                   pl.BlockSpec((None, N_HEADS, tm, QK_NOPE), head),
                   pl.BlockSpec((None, tm, LANES), lambda b, i: (b, i, 0)),
                   pl.BlockSpec((None, N_HEADS, tm, V_HEAD), head)],
        out_shape=[jax.ShapeDtypeStruct((nb, N_HEADS, seq, 2 * LANES), BF16),
                   jax.ShapeDtypeStruct((nb, N_HEADS, seq, QK_NOPE), BF16),
                   jax.ShapeDtypeStruct((nb, seq, LANES), BF16),
                   jax.ShapeDtypeStruct((nb, N_HEADS, seq, V_HEAD), BF16)],
        compiler_params=pltpu.CompilerParams(
            dimension_semantics=("parallel", "parallel"), vmem_limit_bytes=VMEM_LIMIT_BYTES),
        name="mla_proj",
    )(h, cos_t, sin_t, kv_wa, kv_g, kv_wb, q_wa, q_g, q_wb)


def _rot_cols(w):
    half = QK_ROPE // 2
    return jnp.concatenate([-w[..., half:], w[..., :half]], axis=-1)


def _attn_kernel(q_ref, kn_ref, kr_ref, v_ref, o_ref, s_ref, m_ref, acc_ref):
    tk = ATTN_TK
    qi = pl.program_id(2)
    m_ref[...] = jnp.full_like(m_ref, NEG_INF)
    acc_ref[...] = jnp.zeros_like(acc_ref)
    ones = jnp.ones((tk, LANES), BF16)

    def kv_rows(j):
        return pl.ds(pl.multiple_of(j * tk, tk), tk)

    def scores(j, slot):
        rows = kv_rows(j)
        k = jnp.concatenate([kn_ref[rows, :], kr_ref[rows, :]], axis=1)
        s_ref[slot] = lax.dot_general(q_ref[...], k, (((1,), (1,)), ((), ())),
                                      preferred_element_type=F32)

    def consume(j, slot, masked):
        s = s_ref[slot]
        if masked:
            q_pos = lax.broadcasted_iota(jnp.int32, s.shape, 0)
            k_pos = lax.broadcasted_iota(jnp.int32, s.shape, 1)
            s = jnp.where(k_pos <= q_pos, s, NEG_INF)
        m = m_ref[...]
        m_new = jnp.maximum(m, jnp.max(s, axis=-1, keepdims=True))
        alpha = jnp.exp(m - m_new)
        p = jnp.exp(s - jnp.concatenate([m_new] * (tk // LANES), axis=1)).astype(BF16)
        v_ext = jnp.concatenate([v_ref[kv_rows(j), :], ones], axis=1)
        acc_ref[...] = (jnp.concatenate([alpha, alpha], axis=1) * acc_ref[...]
                        + jnp.dot(p, v_ext, preferred_element_type=F32))
        m_ref[...] = m_new

    scores(0, 0)

    def body(jj, carry):
        j = 2 * jj
        scores(j + 1, 1)
        consume(j, 0, masked=False)
        scores(j + 2, 0)
        consume(j + 1, 1, masked=False)
        return carry

    lax.fori_loop(0, qi // 2, body, 0)

    @pl.when(qi % 2 == 0)
    def _():
        consume(qi, 0, masked=True)

    @pl.when(qi % 2 == 1)
    def _():
        scores(qi, 1)
        consume(qi - 1, 0, masked=False)
        consume(qi, 1, masked=True)

    acc = acc_ref[...]
    o_ref[...] = (acc[:, :V_HEAD] / acc[:, V_HEAD:]).astype(o_ref.dtype)


def _attention(q, kn, kr, v, tq=ATTN_TQ):
    nb, nh, seq, _ = q.shape
    assert tq == ATTN_TK
    return pl.pallas_call(
        _attn_kernel,
        grid=(nb, nh, seq // tq),
        in_specs=[pl.BlockSpec((None, None, tq, 2 * LANES), lambda b, h, i: (b, h, i, 0)),
                  pl.BlockSpec((None, None, seq, QK_NOPE), lambda b, h, i: (b, h, 0, 0)),
                  pl.BlockSpec((None, seq, LANES), lambda b, h, i: (b, 0, 0)),
                  pl.BlockSpec((None, None, seq, V_HEAD), lambda b, h, i: (b, h, 0, 0))],
        out_specs=pl.BlockSpec((None, tq, V_HEAD), lambda b, h, i: (b, i, h)),
        out_shape=jax.ShapeDtypeStruct((nb, seq, nh * V_HEAD), BF16),
        scratch_shapes=[pltpu.VMEM((2, tq, ATTN_TK), F32), pltpu.VMEM((tq, LANES), F32),
                        pltpu.VMEM((tq, V_HEAD + LANES), F32)],
        compiler_params=pltpu.CompilerParams(
            dimension_semantics=("parallel", "parallel", "arbitrary"),
            vmem_limit_bytes=VMEM_LIMIT_BYTES),
        name="attention",
    )(q, kn, kr, v)


def _rope_tables(positions):
    half = QK_ROPE // 2
    inv_freq = ROPE_THETA ** (-jnp.arange(half, dtype=F32) / half)
    ang = positions.astype(F32)[..., None] * inv_freq
    zeros = jnp.zeros(ang.shape[:-1] + (LANES - QK_ROPE,), F32)
    cos_t = jnp.concatenate([jnp.cos(ang), jnp.cos(ang), zeros], axis=-1)
    sin_t = jnp.concatenate([jnp.sin(ang), jnp.sin(ang), zeros], axis=-1)
    return cos_t, sin_t


def _row(v):
    return v.reshape(1, -1).astype(F32)


def kernel(x, positions, ln_mix_g, ln_mix_b, ln_ffn_g, ln_ffn_b, w_ff1, w_ff2, ssm_lam_re, ssm_lam_im, ssm_log_dt, ssm_b_re, ssm_b_im, ssm_c_re, ssm_c_im, ssm_d, ssm_w_glu, ssm_w_out, kv_w_a, kv_norm_g, kv_w_b, q_w_a, q_norm_g, q_w_b, attn_w_o):
    nb, seq, _ = x.shape
    m = nb * seq
    cos_t, sin_t = _rope_tables(positions)
    cos_t = cos_t.reshape(m, LANES)
    sin_t = sin_t.reshape(m, LANES)

    kv_wa = jnp.concatenate([kv_w_a, _rot_cols(kv_w_a[:, KV_LORA:])], axis=1).astype(BF16)
    kv_wb3 = kv_w_b.reshape(KV_LORA, N_HEADS, QK_NOPE + V_HEAD)
    kv_wb = jnp.concatenate([kv_wb3[:, :, :QK_NOPE].reshape(KV_LORA, -1),
                             kv_wb3[:, :, QK_NOPE:].reshape(KV_LORA, -1)], axis=1).astype(BF16)

    h = x.reshape(m, D_MODEL)
    kn = kr = v = None
    for layer in range(DEPTH):
        if layer < N_A_LAYERS:
            i = layer
            wb, wc, a_re, a_im = _s5_params(ssm_lam_re[i], ssm_lam_im[i], ssm_log_dt[i],
                                            ssm_b_re[i], ssm_b_im[i], ssm_c_re[i], ssm_c_im[i])
            y = _s5_scan(h.reshape(nb, seq, D_MODEL), wb, wc, a_re, a_im, _row(ssm_d[i]))
            h = _glu_out_ln(y.reshape(m, D_MODEL), h, ssm_w_glu[i].astype(BF16),
                            ssm_w_out[i].astype(BF16), _row(ln_mix_g[layer]), _row(ln_mix_b[layer]))
        else:
            j = layer - N_A_LAYERS
            qwb3 = q_w_b[j].reshape(Q_LORA, N_HEADS, QK_NOPE + QK_ROPE)
            q_wb = jnp.concatenate([qwb3, _rot_cols(qwb3[:, :, QK_NOPE:])], axis=2)
            q_wb = q_wb.reshape(Q_LORA, N_HEADS * 2 * LANES).astype(BF16)
            q, kn_j, kr_j, v_j = _mla_proj(h, cos_t, sin_t, kv_wa, _row(kv_norm_g), kv_wb,
                                           q_w_a[j].astype(BF16), _row(q_norm_g[j]), q_wb, nb, seq)
            if layer == N_A_LAYERS:
                kn, kr, v = kn_j, kr_j, v_j
            o = _attention(q, kn, kr, v)
            h = _linear_ln(o.reshape(m, N_HEADS * V_HEAD), h, attn_w_o[j].astype(BF16),
                           _row(ln_mix_g[layer]), _row(ln_mix_b[layer]))
        h = _mlp_ln(h, w_ff1[layer].astype(BF16), w_ff2[layer].astype(BF16),
                    _row(ln_ffn_g[layer]), _row(ln_ffn_b[layer]))
    return h.reshape(nb, seq, D_MODEL)
```

```python
import math

import jax
import jax.numpy as jnp
from jax import lax
from jax.experimental import pallas as pl
from jax.experimental.pallas import tpu as pltpu

F32 = jnp.float32
BF16 = jnp.bfloat16

D_MODEL = 1024
DEPTH = 2
SSM_GROUP = 16
N_GROUPS = D_MODEL // SSM_GROUP
SSM_STATE = 64
N_HEADS = 8
QK_NOPE = 128
QK_ROPE = 64
V_HEAD = 128
Q_LORA = 384
KV_LORA = 256
ROPE_THETA = 10000.0
SM_SCALE = (QK_NOPE + QK_ROPE) ** -0.5
NEG_INF = -1e30
D_FF = 4 * D_MODEL
N_A_LAYERS = DEPTH // 2
DN_ALPHA = (2 * DEPTH) ** 0.25
LN_EPS = 1e-5
RMS_EPS = 1e-6

LANES = 128
SUBLANES = 8
MXU_DIM = 256
VMEM_LIMIT_BYTES = 56 * 1024 * 1024

S5_GROUPS_PER_TILE = MXU_DIM // SSM_GROUP
S5_N_TILES = N_GROUPS // S5_GROUPS_PER_TILE
S5_TILE_COLS = 2 * S5_GROUPS_PER_TILE * SSM_STATE
S5_SLABS_PER_TILE = S5_TILE_COLS // LANES
S5_PAIRS_PER_TILE = S5_SLABS_PER_TILE // 2
S5_N_SLABS = S5_N_TILES * S5_SLABS_PER_TILE
S5_T = 64
S5_PITCH = S5_T + SUBLANES

ROW_TILE = 512
ATTN_TQ = 512


def _layer_norm(x, g, b):
    mu = jnp.mean(x, axis=-1, keepdims=True)
    xc = x - mu
    var = jnp.mean(xc * xc, axis=-1, keepdims=True)
    return xc * lax.rsqrt(var + LN_EPS) * g + b


def _rms_norm(x, g):
    return x * lax.rsqrt(jnp.mean(x * x, axis=-1, keepdims=True) + RMS_EPS) * g


def _gelu_tanh(x):
    c = math.sqrt(2.0 / math.pi)
    return 0.5 * x * (1.0 + jnp.tanh(c * (x + 0.044715 * (x * x * x))))


def _const_spec(shape):
    nd = len(shape)
    return pl.BlockSpec(shape, lambda *_: (0,) * nd, pipeline_mode=pl.Buffered(1))


def _s5_kernel(x_ref, wb_ref, wc_ref, ar_ref, ai_ref, d_ref, o_ref, xs_ref, st_ref, ys_ref, hc_ref):
    nb, t_len, _ = x_ref.shape
    n_lane_slabs = D_MODEL // LANES

    @pl.when(pl.program_id(0) == 0)
    def _():
        hc_ref[...] = jnp.zeros_like(hc_ref)

    for b in range(nb):
        x_b = x_ref[b]
        for c in range(n_lane_slabs):
            xs_ref[c, pl.ds(b * S5_PITCH, t_len), :] = x_b[:, c * LANES:(c + 1) * LANES]

    def x_time_major(c):
        return jnp.concatenate(
            [xs_ref[c, pl.ds(t, nb, stride=S5_PITCH), :] for t in range(t_len)], axis=0)

    def project(kt):
        cs = (2 * kt, 2 * kt + 1)
        u = [x_time_major(c) for c in cs]
        for c, u_c in zip(cs, u):
            ys_ref[c] = d_ref[:, c * LANES:(c + 1) * LANES] * u_c
        bu = jnp.dot(jnp.concatenate(u, axis=1).astype(BF16), wb_ref[kt],
                     preferred_element_type=F32)
        for j in range(S5_SLABS_PER_TILE):
            st_ref[kt * S5_SLABS_PER_TILE + j] = bu[:, j * LANES:(j + 1) * LANES]

    def scan(kt):
        base = kt * S5_SLABS_PER_TILE
        np_ = S5_PAIRS_PER_TILE
        a_re = [ar_ref[kt * np_ + j] for j in range(np_)]
        a_im = [ai_ref[kt * np_ + j] for j in range(np_)]
        h = [hc_ref[base + j] for j in range(S5_SLABS_PER_TILE)]
        for t in range(t_len):
            rows_t = pl.ds(t * nb, nb)
            for j in range(np_):
                h_re, h_im = h[j], h[np_ + j]
                n_re = a_re[j] * h_re - a_im[j] * h_im + st_ref[base + j, rows_t, :]
                n_im = a_re[j] * h_im + a_im[j] * h_re + st_ref[base + np_ + j, rows_t, :]
                st_ref[base + j, rows_t, :] = n_re
                st_ref[base + np_ + j, rows_t, :] = n_im
                h[j], h[np_ + j] = n_re, n_im
        for j in range(S5_SLABS_PER_TILE):
            hc_ref[base + j] = h[j]

    def read_out(kt):
        h_tile = jnp.concatenate(
            [st_ref[kt * S5_SLABS_PER_TILE + j].astype(BF16) for j in range(S5_SLABS_PER_TILE)],
            axis=1)
        y = jnp.dot(h_tile, wc_ref[kt], preferred_element_type=F32)
        for c2 in range(2):
            c = 2 * kt + c2
            ys_ref[c] = _gelu_tanh(y[:, c2 * LANES:(c2 + 1) * LANES] + ys_ref[c])
            for b in range(nb):
                o_ref[b, :, c * LANES:(c + 1) * LANES] = (
                    ys_ref[c, pl.ds(b, t_len, stride=nb), :].astype(o_ref.dtype))

    stages = (project, scan, read_out)
    for step in range(S5_N_TILES + len(stages) - 1):
        for depth, stage in enumerate(stages):
            kt = step - depth
            if 0 <= kt < S5_N_TILES:
                stage(kt)


def _s5_scan(x, wb, wc, a_re, a_im, d_skip, t_len=S5_T):
    nb, seq, _ = x.shape
    assert nb == SUBLANES and seq % t_len == 0 and t_len % SUBLANES == 0
    rows = nb * t_len
    return pl.pallas_call(
        _s5_kernel,
        grid=(seq // t_len,),
        in_specs=[
            pl.BlockSpec((nb, t_len, D_MODEL), lambda i: (0, i, 0)),
            _const_spec(wb.shape),
            _const_spec(wc.shape),
            _const_spec(a_re.shape),
            _const_spec(a_im.shape),
            _const_spec(d_skip.shape),
        ],
        out_specs=pl.BlockSpec((nb, t_len, D_MODEL), lambda i: (0, i, 0)),
        out_shape=jax.ShapeDtypeStruct(x.shape, BF16),
        scratch_shapes=[
            pltpu.VMEM((D_MODEL // LANES, nb * S5_PITCH, LANES), F32),
            pltpu.VMEM((S5_N_SLABS, rows, LANES), F32),
            pltpu.VMEM((D_MODEL // LANES, rows, LANES), F32),
            pltpu.VMEM((S5_N_SLABS, SUBLANES, LANES), F32),
        ],
        compiler_params=pltpu.CompilerParams(
            dimension_semantics=("arbitrary",), vmem_limit_bytes=VMEM_LIMIT_BYTES),
        name="s5_scan",
    )(x, wb, wc, a_re, a_im, d_skip)


def _s5_params(lam_re, lam_im, log_dt, b_re, b_im, c_re, c_im):
    dt = jnp.exp(log_dt)[:, None]
    mag = jnp.exp(lam_re * dt)
    a_re = mag * jnp.cos(lam_im * dt)
    a_im = mag * jnp.sin(lam_im * dt)
    inv_den = 1.0 / (lam_re * lam_re + lam_im * lam_im)
    coef_re = ((a_re - 1.0) * lam_re + a_im * lam_im) * inv_den
    coef_im = (a_im * lam_re - (a_re - 1.0) * lam_im) * inv_den
    bb_re = coef_re[..., None] * b_re - coef_im[..., None] * b_im
    bb_im = coef_re[..., None] * b_im + coef_im[..., None] * b_re
    gpt = S5_GROUPS_PER_TILE
    eye = jnp.eye(gpt, dtype=F32)

    def in_block(bb):
        bb = bb.reshape(S5_N_TILES, gpt, SSM_STATE, SSM_GROUP)
        blk = jnp.einsum('kgpc,gh->kgchp', bb, eye)
        return blk.reshape(S5_N_TILES, gpt * SSM_GROUP, gpt * SSM_STATE)

    def out_block(cc):
        cc = cc.reshape(S5_N_TILES, gpt, SSM_GROUP, SSM_STATE)
        blk = jnp.einsum('kgcp,gh->kgphc', cc, eye)
        return blk.reshape(S5_N_TILES, gpt * SSM_STATE, gpt * SSM_GROUP)

    wb = jnp.concatenate([in_block(bb_re), in_block(bb_im)], axis=2).astype(BF16)
    wc = jnp.concatenate([out_block(c_re), out_block(-c_im)], axis=1).astype(BF16)
    n_pairs = S5_N_SLABS // 2

    def bcast(a):
        return jnp.broadcast_to(a.reshape(n_pairs, 1, LANES), (n_pairs, SUBLANES, LANES))

    return wb, wc, bcast(a_re), bcast(a_im)


def _mlp_ln_value(h, w1_ref, w2_ref, g, b):
    hb = h.astype(BF16)
    acc = DN_ALPHA * h
    for c in range(D_FF // D_MODEL):
        cols = slice(c * D_MODEL, (c + 1) * D_MODEL)
        u = jnp.maximum(jnp.dot(hb, w1_ref[:, cols], preferred_element_type=F32), 0.0)
        acc = acc + jnp.dot((u * u).astype(BF16), w2_ref[cols, :], preferred_element_type=F32)
    return _layer_norm(acc, g, b)


def _s5_tail_mlp_kernel(y_ref, x_ref, wg_ref, wo_ref, g1_ref, b1_ref, w1_ref, w2_ref, g2_ref, b2_ref,
                        o_ref):
    z = jnp.dot(y_ref[...], wg_ref[...], preferred_element_type=F32)
    val, gate = z[:, :D_MODEL], z[:, D_MODEL:]
    m = (val * jax.nn.sigmoid(gate)).astype(BF16)
    mix = jnp.dot(m, wo_ref[...], preferred_element_type=F32)
    h1 = _layer_norm(DN_ALPHA * x_ref[...] + mix, g1_ref[...], b1_ref[...])
    o_ref[...] = _mlp_ln_value(h1, w1_ref, w2_ref, g2_ref[...], b2_ref[...])


def _attn_out_mlp_kernel(a_ref, r_ref, wo_ref, g1_ref, b1_ref, w1_ref, w2_ref, g2_ref, b2_ref, o_ref):
    mix = jnp.dot(a_ref[...], wo_ref[...], preferred_element_type=F32)
    h3 = _layer_norm(DN_ALPHA * r_ref[...] + mix, g1_ref[...], b1_ref[...])
    o_ref[...] = _mlp_ln_value(h3, w1_ref, w2_ref, g2_ref[...], b2_ref[...])


def _row_stage(body, name, row_inputs, const_inputs, w1_all, w2_all, layer, ln2, tm=ROW_TILE):
    m = row_inputs[0].shape[0]
    row = lambda i: (i, 0)
    layer_spec = lambda w: pl.BlockSpec((None,) + w.shape[1:], lambda i: (layer, 0, 0),
                                        pipeline_mode=pl.Buffered(1))
    return pl.pallas_call(
        body,
        grid=(m // tm,),
        in_specs=([pl.BlockSpec((tm, a.shape[1]), row) for a in row_inputs]
                  + [_const_spec(c.shape) for c in const_inputs]
                  + [layer_spec(w1_all), layer_spec(w2_all)]
                  + [_const_spec(c.shape) for c in ln2]),
        out_specs=pl.BlockSpec((tm, D_MODEL), row),
        out_shape=jax.ShapeDtypeStruct((m, D_MODEL), F32),
        compiler_params=pltpu.CompilerParams(
            dimension_semantics=("parallel",), vmem_limit_bytes=VMEM_LIMIT_BYTES),
        name=name,
    )(*row_inputs, *const_inputs, w1_all, w2_all, *ln2)


def _rope_tile(t2, cos_t, sin_t):
    return t2 * cos_t + pltpu.roll(t2, LANES // 2, axis=1) * sin_t


def _mla_proj_kernel(h_ref, pos_ref, invf_ref, kvwa_ref, kvg_ref, kvwb_ref, qwa_ref, qg_ref, qwb_ref,
                     q_ref, kn_ref, kr_ref, v_ref):
    hb = h_ref[...].astype(BF16)
    ang = pos_ref[...] * invf_ref[...]
    live = lax.broadcasted_iota(jnp.int32, ang.shape, 1) < QK_ROPE
    cos_t = jnp.where(live, jnp.cos(ang), 0.0)
    sin_t = jnp.where(live, jnp.sin(ang), 0.0)
    ckv = jnp.dot(hb, kvwa_ref[...], preferred_element_type=F32)
    kr_ref[...] = _rope_tile(ckv[:, KV_LORA:], cos_t, sin_t).astype(kr_ref.dtype)
    c_kv = _rms_norm(ckv[:, :KV_LORA], kvg_ref[...]).astype(BF16)
    kv = jnp.dot(c_kv, kvwb_ref[...], preferred_element_type=F32)
    for hd in range(N_HEADS):
        kn_ref[hd] = kv[:, hd * QK_NOPE:(hd + 1) * QK_NOPE].astype(kn_ref.dtype)
        v_ref[hd] = kv[:, N_HEADS * QK_NOPE + hd * V_HEAD:
                       N_HEADS * QK_NOPE + (hd + 1) * V_HEAD].astype(v_ref.dtype)
    c_q = _rms_norm(jnp.dot(hb, qwa_ref[...], preferred_element_type=F32), qg_ref[...]).astype(BF16)
    q = jnp.dot(c_q, qwb_ref[...], preferred_element_type=F32) * SM_SCALE
    for hd in range(N_HEADS):
        base = hd * 2 * LANES
        q_ref[hd, :, :LANES] = q[:, base:base + LANES].astype(q_ref.dtype)
        q_ref[hd, :, LANES:] = _rope_tile(q[:, base + LANES:base + 2 * LANES],
                                          cos_t, sin_t).astype(q_ref.dtype)


def _mla_proj(h, pos, inv_freq, kv_wa, kv_g, kv_wb, q_wa, q_g, q_wb, nb, seq, tm=ROW_TILE):
    nt = seq // tm
    row = lambda b, i: (b * nt + i, 0)
    head = lambda b, i: (b, 0, i, 0)
    return pl.pallas_call(
        _mla_proj_kernel,
        grid=(nb, nt),
        in_specs=[pl.BlockSpec((tm, D_MODEL), row), pl.BlockSpec((tm, 1), row),
                  _const_spec(inv_freq.shape),
                  _const_spec(kv_wa.shape), _const_spec(kv_g.shape), _const_spec(kv_wb.shape),
                  _const_spec(q_wa.shape), _const_spec(q_g.shape), _const_spec(q_wb.shape)],
        out_specs=[pl.BlockSpec((None, N_HEADS, tm, 2 * LANES), head),
                   pl.BlockSpec((None, N_HEADS, tm, QK_NOPE), head),
                   pl.BlockSpec((None, tm, LANES), lambda b, i: (b, i, 0)),
                   pl.BlockSpec((None, N_HEADS, tm, V_HEAD), head)],
        out_shape=[jax.ShapeDtypeStruct((nb, N_HEADS, seq, 2 * LANES), BF16),
                   jax.ShapeDtypeStruct((nb, N_HEADS, seq, QK_NOPE), BF16),
                   jax.ShapeDtypeStruct((nb, seq, LANES), BF16),
                   jax.ShapeDtypeStruct((nb, N_HEADS, seq, V_HEAD), BF16)],
        compiler_params=pltpu.CompilerParams(
            dimension_semantics=("parallel", "parallel"), vmem_limit_bytes=VMEM_LIMIT_BYTES),
        name="mla_proj",
    )(h, pos, inv_freq, kv_wa, kv_g, kv_wb, q_wa, q_g, q_wb)


def _rot_cols(w):
    half = QK_ROPE // 2
    return jnp.concatenate([-w[..., half:], w[..., :half]], axis=-1)


def _attn_kernel(q_ref, kn_ref, kr_ref, v_ref, o_ref, s_ref, m_ref, acc_ref):
    tq = tk = ATTN_TQ
    half = tq // 2
    n_q = q_ref.shape[0] // tq
    blocks = [(qi, j) for qi in range(n_q) for j in range(qi + 1)]
    nt = (((1,), (1,)), ((), ()))
    ones = jnp.ones((tk, LANES), BF16)

    def pieces(qi, j):
        return ((0, half, half), (half, tq, tk)) if j == qi else ((0, tq, tk),)

    def scores(n):
        qi, j = blocks[n]
        k = jnp.concatenate([kn_ref[j * tk:(j + 1) * tk, :], kr_ref[j * tk:(j + 1) * tk, :]], axis=1)
        for r0, r1, nk in pieces(qi, j):
            s_ref[n % 2, r0:r1, :nk] = lax.dot_general(
                q_ref[qi * tq + r0:qi * tq + r1, :], k[:nk], nt, preferred_element_type=F32)

    def consume(n):
        qi, j = blocks[n]
        buf = qi % 2
        for r0, r1, nk in pieces(qi, j):
            s = s_ref[n % 2, r0:r1, :nk]
            if j == qi:
                sq = s[:, nk - half:]
                keep = (lax.broadcasted_iota(jnp.int32, sq.shape, 1)
                        <= lax.broadcasted_iota(jnp.int32, sq.shape, 0))
                sq = jnp.where(keep, sq, NEG_INF)
                s = sq if nk == half else jnp.concatenate([s[:, :nk - half], sq], axis=1)
            m_new = jnp.max(s, axis=-1, keepdims=True)
            if j == 0:
                m_new = jnp.broadcast_to(m_new, (r1 - r0, LANES))
            else:
                m_old = m_ref[buf, r0:r1, :]
                m_new = jnp.maximum(m_old, m_new)
            p = jnp.exp(s - jnp.concatenate([m_new] * (nk // LANES), axis=1)).astype(BF16)
            v_ext = jnp.concatenate([v_ref[j * tk:j * tk + nk, :], ones[:nk]], axis=1)
            acc = jnp.dot(p, v_ext, preferred_element_type=F32)
            if j > 0:
                alpha = jnp.exp(m_old - m_new)
                acc = jnp.concatenate([alpha, alpha], axis=1) * acc_ref[buf, r0:r1, :] + acc
            if j == qi:
                o_ref[qi * tq + r0:qi * tq + r1, :] = (
                    acc[:, :V_HEAD] / acc[:, V_HEAD:]).astype(o_ref.dtype)
            else:
                acc_ref[buf, r0:r1, :] = acc
                m_ref[buf, r0:r1, :] = m_new

    scores(0)
    for n in range(len(blocks)):
        if n + 1 < len(blocks):
            scores(n + 1)
        consume(n)


def _attention(q, kn, kr, v):
    nb, nh, seq, _ = q.shape
    assert seq % ATTN_TQ == 0
    return pl.pallas_call(
        _attn_kernel,
        grid=(nb, nh),
        in_specs=[pl.BlockSpec((None, None, seq, 2 * LANES), lambda b, h: (b, h, 0, 0)),
                  pl.BlockSpec((None, None, seq, QK_NOPE), lambda b, h: (b, h, 0, 0)),
                  pl.BlockSpec((None, seq, LANES), lambda b, h: (b, 0, 0)),
                  pl.BlockSpec((None, None, seq, V_HEAD), lambda b, h: (b, h, 0, 0))],
        out_specs=pl.BlockSpec((None, seq, V_HEAD), lambda b, h: (b, 0, h)),
        out_shape=jax.ShapeDtypeStruct((nb, seq, nh * V_HEAD), BF16),
        scratch_shapes=[pltpu.VMEM((2, ATTN_TQ, ATTN_TQ), F32),
                        pltpu.VMEM((2, ATTN_TQ, LANES), F32),
                        pltpu.VMEM((2, ATTN_TQ, V_HEAD + LANES), F32)],
        compiler_params=pltpu.CompilerParams(
            dimension_semantics=("parallel", "parallel"), vmem_limit_bytes=VMEM_LIMIT_BYTES),
        name="attention",
    )(q, kn, kr, v)


def _rope_inv_freq():
    half = QK_ROPE // 2
    inv_freq = ROPE_THETA ** (-jnp.arange(half, dtype=F32) / half)
    zeros = jnp.zeros((LANES - QK_ROPE,), F32)
    return jnp.concatenate([inv_freq, inv_freq, zeros]).reshape(1, LANES)


def _row(v):
    return v.reshape(1, -1).astype(F32)


def kernel(x, positions, ln_mix_g, ln_mix_b, ln_ffn_g, ln_ffn_b, w_ff1, w_ff2, ssm_lam_re, ssm_lam_im, ssm_log_dt, ssm_b_re, ssm_b_im, ssm_c_re, ssm_c_im, ssm_d, ssm_w_glu, ssm_w_out, kv_w_a, kv_norm_g, kv_w_b, q_w_a, q_norm_g, q_w_b, attn_w_o):
    nb, seq, _ = x.shape
    m = nb * seq
    pos = positions.astype(F32).reshape(m, 1)
    inv_freq = _rope_inv_freq()
    w1_all = w_ff1.astype(BF16)
    w2_all = w_ff2.astype(BF16)

    kv_wa = jnp.concatenate([kv_w_a, _rot_cols(kv_w_a[:, KV_LORA:])], axis=1).astype(BF16)
    kv_wb3 = kv_w_b.reshape(KV_LORA, N_HEADS, QK_NOPE + V_HEAD)
    kv_wb = jnp.concatenate([kv_wb3[:, :, :QK_NOPE].reshape(KV_LORA, -1),
                             kv_wb3[:, :, QK_NOPE:].reshape(KV_LORA, -1)], axis=1).astype(BF16)

    h = x.reshape(m, D_MODEL)
    kn = kr = v = None
    for layer in range(DEPTH):
        ln1 = (_row(ln_mix_g[layer]), _row(ln_mix_b[layer]))
        ln2 = (_row(ln_ffn_g[layer]), _row(ln_ffn_b[layer]))
        if layer < N_A_LAYERS:
            i = layer
            wb, wc, a_re, a_im = _s5_params(ssm_lam_re[i], ssm_lam_im[i], ssm_log_dt[i],
                                            ssm_b_re[i], ssm_b_im[i], ssm_c_re[i], ssm_c_im[i])
            y = _s5_scan(h.reshape(nb, seq, D_MODEL), wb, wc, a_re, a_im, _row(ssm_d[i]))
            h = _row_stage(_s5_tail_mlp_kernel, "s5_tail_mlp", (y.reshape(m, D_MODEL), h),
                           (ssm_w_glu[i].astype(BF16), ssm_w_out[i].astype(BF16)) + ln1,
                           w1_all, w2_all, layer, ln2)
        else:
            j = layer - N_A_LAYERS
            qwb3 = q_w_b[j].reshape(Q_LORA, N_HEADS, QK_NOPE + QK_ROPE)
            q_wb = jnp.concatenate([qwb3, _rot_cols(qwb3[:, :, QK_NOPE:])], axis=2)
            q_wb = q_wb.reshape(Q_LORA, N_HEADS * 2 * LANES).astype(BF16)
            q, kn_j, kr_j, v_j = _mla_proj(h, pos, inv_freq, kv_wa, _row(kv_norm_g), kv_wb,
                                           q_w_a[j].astype(BF16), _row(q_norm_g[j]), q_wb, nb, seq)
            if layer == N_A_LAYERS:
                kn, kr, v = kn_j, kr_j, v_j
            o = _attention(q, kn, kr, v)
            h = _row_stage(_attn_out_mlp_kernel, "attn_out_mlp", (o.reshape(m, N_HEADS * V_HEAD), h),
                           (attn_w_o[j].astype(BF16),) + ln1, w1_all, w2_all, layer, ln2)
    return h.reshape(nb, seq, D_MODEL)
```

```python
import math

import jax
import jax.numpy as jnp
from jax import lax
from jax.experimental import pallas as pl
from jax.experimental.pallas import tpu as pltpu

F32 = jnp.float32
BF16 = jnp.bfloat16

D_MODEL = 1024
DEPTH = 2
SSM_GROUP = 16
N_GROUPS = D_MODEL // SSM_GROUP
SSM_STATE = 64
N_HEADS = 8
QK_NOPE = 128
QK_ROPE = 64
V_HEAD = 128
Q_LORA = 384
KV_LORA = 256
ROPE_THETA = 10000.0
SM_SCALE = (QK_NOPE + QK_ROPE) ** -0.5
NEG_INF = -1e30
D_FF = 4 * D_MODEL
N_A_LAYERS = DEPTH // 2
DN_ALPHA = (2 * DEPTH) ** 0.25
LN_EPS = 1e-5
RMS_EPS = 1e-6

LANES = 128
SUBLANES = 8
MXU_DIM = 256
VMEM_LIMIT_BYTES = 56 * 1024 * 1024

S5_GROUPS_PER_TILE = MXU_DIM // SSM_GROUP
S5_N_TILES = N_GROUPS // S5_GROUPS_PER_TILE
S5_TILE_COLS = 2 * S5_GROUPS_PER_TILE * SSM_STATE
S5_SLABS_PER_TILE = S5_TILE_COLS // LANES
S5_PAIRS_PER_TILE = S5_SLABS_PER_TILE // 2
S5_N_SLABS = S5_N_TILES * S5_SLABS_PER_TILE
S5_T = 64
S5_PITCH = S5_T + SUBLANES

ROW_TILE = 512
ATTN_TQ = 512


def _layer_norm(x, g, b):
    mu = jnp.mean(x, axis=-1, keepdims=True)
    xc = x - mu
    var = jnp.mean(xc * xc, axis=-1, keepdims=True)
    return xc * lax.rsqrt(var + LN_EPS) * g + b


def _rms_norm(x, g):
    return x * lax.rsqrt(jnp.mean(x * x, axis=-1, keepdims=True) + RMS_EPS) * g


def _gelu_tanh(x):
    c = math.sqrt(2.0 / math.pi)
    return 0.5 * x * (1.0 + jnp.tanh(c * (x + 0.044715 * (x * x * x))))


def _const_spec(shape):
    nd = len(shape)
    return pl.BlockSpec(shape, lambda *_: (0,) * nd, pipeline_mode=pl.Buffered(1))


def _s5_kernel(x_ref, wb_ref, wc_ref, ar_ref, ai_ref, d_ref, o_ref, xs_ref, st_ref, ys_ref, hc_ref):
    nb, t_len, _ = x_ref.shape
    n_lane_slabs = D_MODEL // LANES

    @pl.when(pl.program_id(0) == 0)
    def _():
        hc_ref[...] = jnp.zeros_like(hc_ref)

    for b in range(nb):
        x_b = x_ref[b]
        for c in range(n_lane_slabs):
            xs_ref[c, pl.ds(b * S5_PITCH, t_len), :] = x_b[:, c * LANES:(c + 1) * LANES]

    def x_time_major(c):
        return jnp.concatenate(
            [xs_ref[c, pl.ds(t, nb, stride=S5_PITCH), :] for t in range(t_len)], axis=0)

    def project(kt):
        cs = (2 * kt, 2 * kt + 1)
        u = [x_time_major(c) for c in cs]
        for c, u_c in zip(cs, u):
            ys_ref[c] = d_ref[:, c * LANES:(c + 1) * LANES] * u_c
        bu = jnp.dot(jnp.concatenate(u, axis=1).astype(BF16), wb_ref[kt],
                     preferred_element_type=F32)
        for j in range(S5_SLABS_PER_TILE):
            st_ref[kt * S5_SLABS_PER_TILE + j] = bu[:, j * LANES:(j + 1) * LANES]

    def scan(kt):
        base = kt * S5_SLABS_PER_TILE
        np_ = S5_PAIRS_PER_TILE
        a_re = [ar_ref[kt * np_ + j] for j in range(np_)]
        a_im = [ai_ref[kt * np_ + j] for j in range(np_)]
        h = [hc_ref[base + j] for j in range(S5_SLABS_PER_TILE)]
        for t in range(t_len):
            rows_t = pl.ds(t * nb, nb)
            for j in range(np_):
                h_re, h_im = h[j], h[np_ + j]
                n_re = a_re[j] * h_re - a_im[j] * h_im + st_ref[base + j, rows_t, :]
                n_im = a_re[j] * h_im + a_im[j] * h_re + st_ref[base + np_ + j, rows_t, :]
                st_ref[base + j, rows_t, :] = n_re
                st_ref[base + np_ + j, rows_t, :] = n_im
                h[j], h[np_ + j] = n_re, n_im
        for j in range(S5_SLABS_PER_TILE):
            hc_ref[base + j] = h[j]

    def read_out(kt):
        h_tile = jnp.concatenate(
            [st_ref[kt * S5_SLABS_PER_TILE + j].astype(BF16) for j in range(S5_SLABS_PER_TILE)],
            axis=1)
        y = jnp.dot(h_tile, wc_ref[kt], preferred_element_type=F32)
        for c2 in range(2):
            c = 2 * kt + c2
            ys_ref[c] = _gelu_tanh(y[:, c2 * LANES:(c2 + 1) * LANES] + ys_ref[c])
            for b in range(nb):
                o_ref[b, :, c * LANES:(c + 1) * LANES] = (
                    ys_ref[c, pl.ds(b, t_len, stride=nb), :].astype(o_ref.dtype))

    stages = (project, scan, read_out)
    for step in range(S5_N_TILES + len(stages) - 1):
        for depth, stage in enumerate(stages):
            kt = step - depth
            if 0 <= kt < S5_N_TILES:
                stage(kt)


def _s5_scan(x, wb, wc, a_re, a_im, d_skip, t_len=S5_T):
    nb, seq, _ = x.shape
    assert nb == SUBLANES and seq % t_len == 0 and t_len % SUBLANES == 0
    rows = nb * t_len
    return pl.pallas_call(
        _s5_kernel,
        grid=(seq // t_len,),
        in_specs=[
            pl.BlockSpec((nb, t_len, D_MODEL), lambda i: (0, i, 0)),
            _const_spec(wb.shape),
            _const_spec(wc.shape),
            _const_spec(a_re.shape),
            _const_spec(a_im.shape),
            _const_spec(d_skip.shape),
        ],
        out_specs=pl.BlockSpec((nb, t_len, D_MODEL), lambda i: (0, i, 0)),
        out_shape=jax.ShapeDtypeStruct(x.shape, BF16),
        scratch_shapes=[
            pltpu.VMEM((D_MODEL // LANES, nb * S5_PITCH, LANES), F32),
            pltpu.VMEM((S5_N_SLABS, rows, LANES), F32),
            pltpu.VMEM((D_MODEL // LANES, rows, LANES), F32),
            pltpu.VMEM((S5_N_SLABS, SUBLANES, LANES), F32),
        ],
        compiler_params=pltpu.CompilerParams(
            dimension_semantics=("arbitrary",), vmem_limit_bytes=VMEM_LIMIT_BYTES),
        name="s5_scan",
    )(x, wb, wc, a_re, a_im, d_skip)


def _s5_params(lam_re, lam_im, log_dt, b_re, b_im, c_re, c_im):
    dt = jnp.exp(log_dt)[:, None]
    mag = jnp.exp(lam_re * dt)
    a_re = mag * jnp.cos(lam_im * dt)
    a_im = mag * jnp.sin(lam_im * dt)
    inv_den = 1.0 / (lam_re * lam_re + lam_im * lam_im)
    coef_re = ((a_re - 1.0) * lam_re + a_im * lam_im) * inv_den
    coef_im = (a_im * lam_re - (a_re - 1.0) * lam_im) * inv_den
    bb_re = coef_re[..., None] * b_re - coef_im[..., None] * b_im
    bb_im = coef_re[..., None] * b_im + coef_im[..., None] * b_re
    gpt = S5_GROUPS_PER_TILE
    eye = jnp.eye(gpt, dtype=F32)

    def in_block(bb):
        bb = bb.reshape(S5_N_TILES, gpt, SSM_STATE, SSM_GROUP)
        blk = jnp.einsum('kgpc,gh->kgchp', bb, eye)
        return blk.reshape(S5_N_TILES, gpt * SSM_GROUP, gpt * SSM_STATE)

    def out_block(cc):
        cc = cc.reshape(S5_N_TILES, gpt, SSM_GROUP, SSM_STATE)
        blk = jnp.einsum('kgcp,gh->kgphc', cc, eye)
        return blk.reshape(S5_N_TILES, gpt * SSM_STATE, gpt * SSM_GROUP)

    wb = jnp.concatenate([in_block(bb_re), in_block(bb_im)], axis=2).astype(BF16)
    wc = jnp.concatenate([out_block(c_re), out_block(-c_im)], axis=1).astype(BF16)
    n_pairs = S5_N_SLABS // 2

    def bcast(a):
        return jnp.broadcast_to(a.reshape(n_pairs, 1, LANES), (n_pairs, SUBLANES, LANES))

    return wb, wc, bcast(a_re), bcast(a_im)


ROW_SUBTILES = 2
MLP_CHUNKS = D_FF // D_MODEL


def _skewed(n_stages, stage):
    for step in range(n_stages + ROW_SUBTILES - 1):
        for r in range(ROW_SUBTILES):
            k = step - r
            if 0 <= k < n_stages:
                stage(r, k)


def _sub_rows(ref, r):
    sub = ref.shape[0] // ROW_SUBTILES
    return slice(r * sub, (r + 1) * sub)


def _mlp_stages(d, k, rows, g1_ref, b1_ref, w1_ref, w2_ref, g2_ref, b2_ref, o_ref):
    if k == 0:
        h = _layer_norm(d.pop("pre"), g1_ref[...], b1_ref[...])
        d["hb"] = h.astype(BF16)
        d["acc"] = DN_ALPHA * h
    elif k <= MLP_CHUNKS:
        cols = slice((k - 1) * D_MODEL, k * D_MODEL)
        u = jnp.maximum(jnp.dot(d["hb"], w1_ref[:, cols], preferred_element_type=F32), 0.0)
        d["acc"] = d["acc"] + jnp.dot((u * u).astype(BF16), w2_ref[cols, :],
                                      preferred_element_type=F32)
    else:
        o_ref[rows, :] = _layer_norm(d.pop("acc"), g2_ref[...], b2_ref[...])


def _s5_tail_mlp_kernel(y_ref, x_ref, wg_ref, wo_ref, g1_ref, b1_ref, w1_ref, w2_ref, g2_ref, b2_ref,
                        o_ref):
    state = [dict() for _ in range(ROW_SUBTILES)]

    def stage(r, k):
        d, rows = state[r], _sub_rows(o_ref, r)
        if k == 0:
            d["z"] = jnp.dot(y_ref[rows, :], wg_ref[...], preferred_element_type=F32)
        elif k == 1:
            z = d.pop("z")
            m = (z[:, :D_MODEL] * jax.nn.sigmoid(z[:, D_MODEL:])).astype(BF16)
            d["pre"] = DN_ALPHA * x_ref[rows, :] + jnp.dot(m, wo_ref[...],
                                                           preferred_element_type=F32)
        else:
            _mlp_stages(d, k - 2, rows, g1_ref, b1_ref, w1_ref, w2_ref, g2_ref, b2_ref, o_ref)

    _skewed(MLP_CHUNKS + 4, stage)


def _attn_out_mlp_kernel(a_ref, r_ref, wo_ref, g1_ref, b1_ref, w1_ref, w2_ref, g2_ref, b2_ref, o_ref):
    state = [dict() for _ in range(ROW_SUBTILES)]

    def stage(r, k):
        d, rows = state[r], _sub_rows(o_ref, r)
        if k == 0:
            d["pre"] = DN_ALPHA * r_ref[rows, :] + jnp.dot(a_ref[rows, :], wo_ref[...],
                                                           preferred_element_type=F32)
        else:
            _mlp_stages(d, k - 1, rows, g1_ref, b1_ref, w1_ref, w2_ref, g2_ref, b2_ref, o_ref)

    _skewed(MLP_CHUNKS + 3, stage)


def _row_stage(body, name, row_inputs, const_inputs, w1_all, w2_all, layer, ln2, tm=ROW_TILE):
    m = row_inputs[0].shape[0]
    row = lambda i: (i, 0)
    layer_spec = lambda w: pl.BlockSpec((None,) + w.shape[1:], lambda i: (layer, 0, 0),
                                        pipeline_mode=pl.Buffered(1))
    return pl.pallas_call(
        body,
        grid=(m // tm,),
        in_specs=([pl.BlockSpec((tm, a.shape[1]), row) for a in row_inputs]
                  + [_const_spec(c.shape) for c in const_inputs]
                  + [layer_spec(w1_all), layer_spec(w2_all)]
                  + [_const_spec(c.shape) for c in ln2]),
        out_specs=pl.BlockSpec((tm, D_MODEL), row),
        out_shape=jax.ShapeDtypeStruct((m, D_MODEL), F32),
        compiler_params=pltpu.CompilerParams(
            dimension_semantics=("parallel",), vmem_limit_bytes=VMEM_LIMIT_BYTES),
        name=name,
    )(*row_inputs, *const_inputs, w1_all, w2_all, *ln2)


ROPE_HALF = QK_ROPE // 2
PROJ_A_COLS = 3 * MXU_DIM
Q_ROPE_TILES = N_HEADS * QK_ROPE // LANES


def _mla_proj_kernel(h_ref, pos_ref, invf_ref, wa_ref, kvg_ref, kvwb_ref, qg_ref, qwb_ref,
                     q_ref, kn_ref, kr_ref, v_ref):
    state = [dict() for _ in range(ROW_SUBTILES)]
    c_q0 = KV_LORA
    k_rope0 = KV_LORA + Q_LORA

    def rope(x, d):
        swapped = jnp.where(d["first"], pltpu.roll(x, LANES - ROPE_HALF, axis=1),
                            pltpu.roll(x, ROPE_HALF, axis=1))
        return x * d["cos"] + swapped * d["sin"]

    def stage(r, k):
        d, rows = state[r], _sub_rows(h_ref, r)
        if k == 0:
            d["a"] = jnp.dot(h_ref[rows, :].astype(BF16), wa_ref[...], preferred_element_type=F32)
            ang = pos_ref[rows, :] * invf_ref[...]
            lane = lax.broadcasted_iota(jnp.int32, ang.shape, 1)
            d["first"] = (lane & ROPE_HALF) == 0
            d["low"] = lane < QK_ROPE
            d["cos"] = jnp.cos(ang)
            d["sin"] = jnp.sin(ang)
        elif k == 1:
            a = d.pop("a")
            kr_ref[rows, :] = rope(a[:, k_rope0:k_rope0 + LANES], d).astype(kr_ref.dtype)
            c_kv = _rms_norm(a[:, :KV_LORA], kvg_ref[...]).astype(BF16)
            d["c_q"] = _rms_norm(a[:, c_q0:c_q0 + Q_LORA], qg_ref[...]).astype(BF16)
            d["kv"] = jnp.dot(c_kv, kvwb_ref[...], preferred_element_type=F32)
        elif k == 2:
            kv = d.pop("kv")
            for hd in range(N_HEADS):
                kn_ref[hd, rows, :] = kv[:, hd * QK_NOPE:(hd + 1) * QK_NOPE].astype(kn_ref.dtype)
                v0 = N_HEADS * QK_NOPE + hd * V_HEAD
                v_ref[hd, rows, :] = kv[:, v0:v0 + V_HEAD].astype(v_ref.dtype)
            d["q"] = jnp.dot(d.pop("c_q"), qwb_ref[...], preferred_element_type=F32) * SM_SCALE
        else:
            q = d.pop("q")
            for hd in range(N_HEADS):
                q_ref[hd, rows, :LANES] = q[:, hd * LANES:(hd + 1) * LANES].astype(q_ref.dtype)
            for p in range(Q_ROPE_TILES):
                c0 = N_HEADS * QK_NOPE + p * LANES
                y = rope(q[:, c0:c0 + LANES], d)
                for hd, y_hd in ((2 * p, y), (2 * p + 1, pltpu.roll(y, QK_ROPE, axis=1))):
                    q_ref[hd, rows, LANES:] = jnp.where(d["low"], y_hd, 0.0).astype(q_ref.dtype)

    _skewed(4, stage)


def _mla_proj(h, pos, inv_freq, w_a, kv_g, kv_wb, q_g, q_wb, nb, seq, tm=ROW_TILE):
    nt = seq // tm
    row = lambda b, i: (b * nt + i, 0)
    head = lambda b, i: (b, 0, i, 0)
    consts = (inv_freq, w_a, kv_g, kv_wb, q_g, q_wb)
    return pl.pallas_call(
        _mla_proj_kernel,
        grid=(nb, nt),
        in_specs=[pl.BlockSpec((tm, D_MODEL), row), pl.BlockSpec((tm, 1), row)]
        + [_const_spec(c.shape) for c in consts],
        out_specs=[pl.BlockSpec((None, N_HEADS, tm, 2 * LANES), head),
                   pl.BlockSpec((None, N_HEADS, tm, QK_NOPE), head),
                   pl.BlockSpec((None, tm, LANES), lambda b, i: (b, i, 0)),
                   pl.BlockSpec((None, N_HEADS, tm, V_HEAD), head)],
        out_shape=[jax.ShapeDtypeStruct((nb, N_HEADS, seq, 2 * LANES), BF16),
                   jax.ShapeDtypeStruct((nb, N_HEADS, seq, QK_NOPE), BF16),
                   jax.ShapeDtypeStruct((nb, seq, LANES), BF16),
                   jax.ShapeDtypeStruct((nb, N_HEADS, seq, V_HEAD), BF16)],
        compiler_params=pltpu.CompilerParams(
            dimension_semantics=("parallel", "parallel"), vmem_limit_bytes=VMEM_LIMIT_BYTES),
        name="mla_proj",
    )(h, pos, *consts)


def _mla_weights(kv_w_a, kv_w_b, q_w_a, q_w_b):
    pad = jnp.zeros((D_MODEL, PROJ_A_COLS - KV_LORA - Q_LORA - QK_ROPE), F32)
    w_a = jnp.concatenate([kv_w_a[:, :KV_LORA], q_w_a, kv_w_a[:, KV_LORA:], pad], axis=1)
    kv_wb3 = kv_w_b.reshape(KV_LORA, N_HEADS, QK_NOPE + V_HEAD)
    kv_wb = jnp.concatenate([kv_wb3[:, :, :QK_NOPE].reshape(KV_LORA, -1),
                             kv_wb3[:, :, QK_NOPE:].reshape(KV_LORA, -1)], axis=1)
    q_wb3 = q_w_b.reshape(Q_LORA, N_HEADS, QK_NOPE + QK_ROPE)
    q_wb = jnp.concatenate([q_wb3[:, :, :QK_NOPE].reshape(Q_LORA, -1),
                            q_wb3[:, :, QK_NOPE:].reshape(Q_LORA, -1)], axis=1)
    return w_a.astype(BF16), kv_wb.astype(BF16), q_wb.astype(BF16)


def _attn_kernel(q_ref, kn_ref, kr_ref, v_ref, o_ref, s_ref, m_ref, acc_ref):
    tq = tk = ATTN_TQ
    half = tq // 2
    n_q = q_ref.shape[0] // tq
    blocks = [(qi, j) for qi in range(n_q) for j in range(qi + 1)]
    nt = (((1,), (1,)), ((), ()))
    ones = jnp.ones((tk, LANES), BF16)

    def pieces(qi, j):
        return ((0, half, half), (half, tq, tk)) if j == qi else ((0, tq, tk),)

    def scores(n):
        qi, j = blocks[n]
        k = jnp.concatenate([kn_ref[j * tk:(j + 1) * tk, :], kr_ref[j * tk:(j + 1) * tk, :]], axis=1)
        for r0, r1, nk in pieces(qi, j):
            s_ref[n % 2, r0:r1, :nk] = lax.dot_general(
                q_ref[qi * tq + r0:qi * tq + r1, :], k[:nk], nt, preferred_element_type=F32)

    def consume(n):
        qi, j = blocks[n]
        buf = qi % 2
        for r0, r1, nk in pieces(qi, j):
            s = s_ref[n % 2, r0:r1, :nk]
            if j == qi:
                sq = s[:, nk - half:]
                keep = (lax.broadcasted_iota(jnp.int32, sq.shape, 1)
                        <= lax.broadcasted_iota(jnp.int32, sq.shape, 0))
                sq = jnp.where(keep, sq, NEG_INF)
                s = sq if nk == half else jnp.concatenate([s[:, :nk - half], sq], axis=1)
            m_new = jnp.max(s, axis=-1, keepdims=True)
            if j == 0:
                m_new = jnp.broadcast_to(m_new, (r1 - r0, LANES))
            else:
                m_old = m_ref[buf, r0:r1, :]
                m_new = jnp.maximum(m_old, m_new)
            p = jnp.exp(s - jnp.concatenate([m_new] * (nk // LANES), axis=1)).astype(BF16)
            v_ext = jnp.concatenate([v_ref[j * tk:j * tk + nk, :], ones[:nk]], axis=1)
            acc = jnp.dot(p, v_ext, preferred_element_type=F32)
            if j > 0:
                alpha = jnp.exp(m_old - m_new)
                acc = jnp.concatenate([alpha, alpha], axis=1) * acc_ref[buf, r0:r1, :] + acc
            if j == qi:
                o_ref[qi * tq + r0:qi * tq + r1, :] = (
                    acc[:, :V_HEAD] / acc[:, V_HEAD:]).astype(o_ref.dtype)
            else:
                acc_ref[buf, r0:r1, :] = acc
                m_ref[buf, r0:r1, :] = m_new

    scores(0)
    for n in range(len(blocks)):
        if n + 1 < len(blocks):
            scores(n + 1)
        consume(n)


def _attention(q, kn, kr, v):
    nb, nh, seq, _ = q.shape
    assert seq % ATTN_TQ == 0
    return pl.pallas_call(
        _attn_kernel,
        grid=(nb, nh),
        in_specs=[pl.BlockSpec((None, None, seq, 2 * LANES), lambda b, h: (b, h, 0, 0)),
                  pl.BlockSpec((None, None, seq, QK_NOPE), lambda b, h: (b, h, 0, 0)),
                  pl.BlockSpec((None, seq, LANES), lambda b, h: (b, 0, 0)),
                  pl.BlockSpec((None, None, seq, V_HEAD), lambda b, h: (b, h, 0, 0))],
        out_specs=pl.BlockSpec((None, seq, V_HEAD), lambda b, h: (b, 0, h)),
        out_shape=jax.ShapeDtypeStruct((nb, seq, nh * V_HEAD), BF16),
        scratch_shapes=[pltpu.VMEM((2, ATTN_TQ, ATTN_TQ), F32),
                        pltpu.VMEM((2, ATTN_TQ, LANES), F32),
                        pltpu.VMEM((2, ATTN_TQ, V_HEAD + LANES), F32)],
        compiler_params=pltpu.CompilerParams(
            dimension_semantics=("parallel", "parallel"), vmem_limit_bytes=VMEM_LIMIT_BYTES),
        name="attention",
    )(q, kn, kr, v)


def _rope_inv_freq():
    inv_freq = ROPE_THETA ** (-jnp.arange(ROPE_HALF, dtype=F32) / ROPE_HALF)
    group = jnp.concatenate([-inv_freq, inv_freq])
    return jnp.tile(group, LANES // QK_ROPE).reshape(1, LANES)


def _row(v):
    return v.reshape(1, -1).astype(F32)


def kernel(x, positions, ln_mix_g, ln_mix_b, ln_ffn_g, ln_ffn_b, w_ff1, w_ff2, ssm_lam_re, ssm_lam_im, ssm_log_dt, ssm_b_re, ssm_b_im, ssm_c_re, ssm_c_im, ssm_d, ssm_w_glu, ssm_w_out, kv_w_a, kv_norm_g, kv_w_b, q_w_a, q_norm_g, q_w_b, attn_w_o):
    nb, seq, _ = x.shape
    m = nb * seq
    pos = positions.astype(F32).reshape(m, 1)
    inv_freq = _rope_inv_freq()
    w1_all = w_ff1.astype(BF16)
    w2_all = w_ff2.astype(BF16)

    h = x.reshape(m, D_MODEL)
    kn = kr = v = None
    for layer in range(DEPTH):
        ln1 = (_row(ln_mix_g[layer]), _row(ln_mix_b[layer]))
        ln2 = (_row(ln_ffn_g[layer]), _row(ln_ffn_b[layer]))
        if layer < N_A_LAYERS:
            i = layer
            wb, wc, a_re, a_im = _s5_params(ssm_lam_re[i], ssm_lam_im[i], ssm_log_dt[i],
                                            ssm_b_re[i], ssm_b_im[i], ssm_c_re[i], ssm_c_im[i])
            y = _s5_scan(h.reshape(nb, seq, D_MODEL), wb, wc, a_re, a_im, _row(ssm_d[i]))
            h = _row_stage(_s5_tail_mlp_kernel, "s5_tail_mlp", (y.reshape(m, D_MODEL), h),
                           (ssm_w_glu[i].astype(BF16), ssm_w_out[i].astype(BF16)) + ln1,
                           w1_all, w2_all, layer, ln2)
        else:
            j = layer - N_A_LAYERS
            w_a, kv_wb, q_wb = _mla_weights(kv_w_a, kv_w_b, q_w_a[j], q_w_b[j])
            q, kn_j, kr_j, v_j = _mla_proj(h, pos, inv_freq, w_a, _row(kv_norm_g), kv_wb,
                                           _row(q_norm_g[j]), q_wb, nb, seq)
            if layer == N_A_LAYERS:
                kn, kr, v = kn_j, kr_j, v_j
            o = _attention(q, kn, kr, v)
            h = _row_stage(_attn_out_mlp_kernel, "attn_out_mlp", (o.reshape(m, N_HEADS * V_HEAD), h),
                           (attn_w_o[j].astype(BF16),) + ln1, w1_all, w2_all, layer, ln2)
    return h.reshape(nb, seq, D_MODEL)
```

```python
import math

import jax
import jax.numpy as jnp
from jax import lax
from jax.experimental import pallas as pl
from jax.experimental.pallas import tpu as pltpu

F32 = jnp.float32
BF16 = jnp.bfloat16

D_MODEL = 1024
DEPTH = 2
SSM_GROUP = 16
N_GROUPS = D_MODEL // SSM_GROUP
SSM_STATE = 64
N_HEADS = 8
QK_NOPE = 128
QK_ROPE = 64
V_HEAD = 128
Q_LORA = 384
KV_LORA = 256
ROPE_THETA = 10000.0
SM_SCALE = (QK_NOPE + QK_ROPE) ** -0.5
NEG_INF = -1e30
D_FF = 4 * D_MODEL
N_A_LAYERS = DEPTH // 2
DN_ALPHA = (2 * DEPTH) ** 0.25
LN_EPS = 1e-5
RMS_EPS = 1e-6

LANES = 128
SUBLANES = 8
MXU_DIM = 256
VMEM_LIMIT_BYTES = 56 * 1024 * 1024

S5_L = 8
S5_T = 256
S5_PAIRS = N_GROUPS // 2
S5_N_SLABS = D_MODEL // LANES
S5_SUPER = 4 * LANES
S5_PITCH = S5_T + SUBLANES

ROW_TILE = 512
ATTN_TQ = 512


def _layer_norm(x, g, b):
    mu = jnp.mean(x, axis=-1, keepdims=True)
    xc = x - mu
    var = jnp.mean(xc * xc, axis=-1, keepdims=True)
    return xc * lax.rsqrt(var + LN_EPS) * g + b


def _rms_norm(x, g):
    return x * lax.rsqrt(jnp.mean(x * x, axis=-1, keepdims=True) + RMS_EPS) * g


def _gelu_tanh(x):
    c = math.sqrt(2.0 / math.pi)
    return 0.5 * x * (1.0 + jnp.tanh(c * (x + 0.044715 * (x * x * x))))


def _const_spec(shape):
    nd = len(shape)
    return pl.BlockSpec(shape, lambda *_: (0,) * nd, pipeline_mode=pl.Buffered(1))


def _s5_kernel(x_ref, perm_ref, permt_ref, win_ref, w5_ref, a8r_ref, a8i_ref, d_ref, o_ref,
               xs_ref, hc_ref):
    nb, t_len, _ = x_ref.shape
    n_chunks = t_len // S5_L
    halves = S5_L * LANES // S5_SUPER

    @pl.when(pl.program_id(0) == 0)
    def _():
        hc_ref[...] = jnp.zeros_like(hc_ref)

    for b in range(nb):
        x_b = x_ref[b]
        for l in range(S5_N_SLABS):
            xs_ref[l, pl.ds(b * S5_PITCH, t_len), :] = x_b[:, l * LANES:(l + 1) * LANES]

    def step_rows(c, j):
        return pl.ds(S5_L * c + j, nb, stride=S5_PITCH)

    for l in range(S5_N_SLABS):
        d_l = d_ref[:, l * LANES:(l + 1) * LANES]
        z_half = []
        for a in range(halves):
            lhs = jnp.concatenate(
                [jnp.concatenate([xs_ref[l, step_rows(c, 4 * a + jj), :] for c in range(n_chunks)],
                                 axis=0) for jj in range(4)], axis=1).astype(BF16)
            z_half.append(jnp.dot(lhs, perm_ref[...], preferred_element_type=F32).astype(BF16))
        y_pair = []
        for gh in range(4):
            q = 4 * l + gh
            z_q = jnp.concatenate([zh[:, gh * LANES:(gh + 1) * LANES] for zh in z_half], axis=1)
            s_q = jnp.dot(z_q, win_ref[q], preferred_element_type=F32)
            a_re, a_im = a8r_ref[q], a8i_ref[q]
            h_re, h_im = hc_ref[2 * q], hc_ref[2 * q + 1]
            ent_re, ent_im = [], []
            for c in range(n_chunks):
                ent_re.append(h_re)
                ent_im.append(h_im)
                rows_c = slice(c * nb, (c + 1) * nb)
                n_re = a_re * h_re - a_im * h_im + s_q[rows_c, :LANES]
                n_im = a_re * h_im + a_im * h_re + s_q[rows_c, LANES:]
                h_re, h_im = n_re, n_im
            hc_ref[2 * q] = h_re
            hc_ref[2 * q + 1] = h_im
            ent = jnp.concatenate([jnp.concatenate(ent_re, axis=0),
                                   jnp.concatenate(ent_im, axis=0)], axis=1).astype(BF16)
            y_pair.append(jnp.dot(jnp.concatenate([z_q, ent], axis=1), w5_ref[q],
                                  preferred_element_type=F32))
        for a in range(halves):
            y_l = jnp.concatenate([y[:, a * LANES:(a + 1) * LANES] for y in y_pair], axis=1)
            out = jnp.dot(y_l.astype(BF16), permt_ref[...], preferred_element_type=F32)
            for jj in range(4):
                for c in range(n_chunks):
                    rows = step_rows(c, 4 * a + jj)
                    xs_ref[l, rows, :] = (out[c * nb:(c + 1) * nb, jj * LANES:(jj + 1) * LANES]
                                          + d_l * xs_ref[l, rows, :])
        for b in range(nb):
            o_ref[b, :, l * LANES:(l + 1) * LANES] = _gelu_tanh(
                xs_ref[l, pl.ds(b * S5_PITCH, t_len), :]).astype(o_ref.dtype)


def _s5_scan(x, params, d_skip, t_len=S5_T):
    nb, seq, _ = x.shape
    assert nb == SUBLANES and seq % t_len == 0 and t_len % S5_L == 0
    return pl.pallas_call(
        _s5_kernel,
        grid=(seq // t_len,),
        in_specs=[pl.BlockSpec((nb, t_len, D_MODEL), lambda i: (0, i, 0))]
        + [_const_spec(p.shape) for p in params] + [_const_spec(d_skip.shape)],
        out_specs=pl.BlockSpec((nb, t_len, D_MODEL), lambda i: (0, i, 0)),
        out_shape=jax.ShapeDtypeStruct(x.shape, BF16),
        scratch_shapes=[
            pltpu.VMEM((S5_N_SLABS, nb * S5_PITCH, LANES), F32),
            pltpu.VMEM((N_GROUPS, SUBLANES, LANES), F32),
        ],
        compiler_params=pltpu.CompilerParams(
            dimension_semantics=("arbitrary",), vmem_limit_bytes=VMEM_LIMIT_BYTES),
        name="s5_scan",
    )(x, *params, d_skip)


def _s5_params(lam_re, lam_im, log_dt, b_re, b_im, c_re, c_im):
    dt = jnp.exp(log_dt)[:, None]
    mag = jnp.exp(lam_re * dt)
    a_re = mag * jnp.cos(lam_im * dt)
    a_im = mag * jnp.sin(lam_im * dt)
    inv_den = 1.0 / (lam_re * lam_re + lam_im * lam_im)
    coef_re = ((a_re - 1.0) * lam_re + a_im * lam_im) * inv_den
    coef_im = (a_im * lam_re - (a_re - 1.0) * lam_im) * inv_den
    bb_re = coef_re[..., None] * b_re - coef_im[..., None] * b_im
    bb_im = coef_re[..., None] * b_im + coef_im[..., None] * b_re
    pw_re, pw_im = [jnp.ones_like(a_re)], [jnp.zeros_like(a_im)]
    for _ in range(S5_L):
        pw_re, pw_im = (pw_re + [pw_re[-1] * a_re - pw_im[-1] * a_im],
                        pw_im + [pw_re[-1] * a_im + pw_im[-1] * a_re])
    pw_re, pw_im = jnp.stack(pw_re), jnp.stack(pw_im)
    ca_re = c_re[None] * pw_re[:, :, None, :] - c_im[None] * pw_im[:, :, None, :]
    ca_im = c_re[None] * pw_im[:, :, None, :] + c_im[None] * pw_re[:, :, None, :]
    pair = lambda t: t.reshape((S5_PAIRS, 2) + t.shape[1:])
    eye2 = jnp.eye(2, dtype=F32)
    steps = jnp.arange(S5_L)

    kern = (jnp.einsum('kgop,gpi->kgoi', ca_re[:S5_L], bb_re)
            - jnp.einsum('kgop,gpi->kgoi', ca_im[:S5_L], bb_im))
    lag = steps[None, :] - steps[:, None]
    toep = jnp.where((lag >= 0)[:, :, None, None, None], kern[jnp.clip(lag, 0)], 0.0)
    toep = pair(jnp.transpose(toep, (2, 0, 4, 1, 3)))
    w_toep = jnp.einsum('qgjiko,gh->qjgikho', toep, eye2).reshape(S5_PAIRS, 2 * LANES, 2 * LANES)

    rev_re, rev_im = pw_re[S5_L - 1 - steps][..., None], pw_im[S5_L - 1 - steps][..., None]
    ab_re = rev_re * bb_re[None] - rev_im * bb_im[None]
    ab_im = rev_re * bb_im[None] + rev_im * bb_re[None]
    ab = pair(jnp.transpose(jnp.stack([ab_re, ab_im]), (2, 1, 4, 0, 3)))
    w_in = jnp.einsum('qgjcrp,gh->qjgcrhp', ab, eye2).reshape(S5_PAIRS, 2 * LANES, 2 * LANES)

    co = jnp.stack([ca_re[1:], -ca_im[1:]])
    co = pair(jnp.transpose(co, (2, 0, 4, 1, 3)))
    w_out = jnp.einsum('qgrpko,gh->qrgpkho', co, eye2).reshape(S5_PAIRS, 2 * LANES, 2 * LANES)
    w5 = jnp.concatenate([w_toep, w_out], axis=1)

    src = jnp.arange(S5_SUPER).reshape(4, 4, 2 * SSM_GROUP)
    dst = jnp.transpose(src, (1, 0, 2)).reshape(-1)
    perm = jnp.zeros((S5_SUPER, S5_SUPER), F32).at[dst, jnp.arange(S5_SUPER)].set(1.0)

    def bcast(a):
        return jnp.broadcast_to(a.reshape(S5_PAIRS, 1, LANES), (S5_PAIRS, SUBLANES, LANES))

    return (perm.astype(BF16), perm.T.astype(BF16), w_in.astype(BF16), w5.astype(BF16),
            bcast(pw_re[S5_L]), bcast(pw_im[S5_L]))


ROW_SUBTILES = 2
MLP_CHUNKS = D_FF // D_MODEL


def _skewed(n_stages, stage):
    for step in range(n_stages + ROW_SUBTILES - 1):
        for r in range(ROW_SUBTILES):
            k = step - r
            if 0 <= k < n_stages:
                stage(r, k)


def _sub_rows(ref, r):
    sub = ref.shape[0] // ROW_SUBTILES
    return slice(r * sub, (r + 1) * sub)


def _mlp_stages(d, k, rows, g1_ref, b1_ref, w1_ref, w2_ref, g2_ref, b2_ref, o_ref):
    if k == 0:
        h = _layer_norm(d.pop("pre"), g1_ref[...], b1_ref[...])
        d["hb"] = h.astype(BF16)
        d["acc"] = DN_ALPHA * h
    elif k <= MLP_CHUNKS:
        cols = slice((k - 1) * D_MODEL, k * D_MODEL)
        u = jnp.maximum(jnp.dot(d["hb"], w1_ref[:, cols], preferred_element_type=F32), 0.0)
        d["acc"] = d["acc"] + jnp.dot((u * u).astype(BF16), w2_ref[cols, :],
                                      preferred_element_type=F32)
    else:
        o_ref[rows, :] = _layer_norm(d.pop("acc"), g2_ref[...], b2_ref[...])


def _s5_tail_mlp_kernel(y_ref, x_ref, wg_ref, wo_ref, g1_ref, b1_ref, w1_ref, w2_ref, g2_ref, b2_ref,
                        o_ref):
    state = [dict() for _ in range(ROW_SUBTILES)]

    def stage(r, k):
        d, rows = state[r], _sub_rows(o_ref, r)
        if k == 0:
            d["z"] = jnp.dot(y_ref[rows, :], wg_ref[...], preferred_element_type=F32)
        elif k == 1:
            z = d.pop("z")
            m = (z[:, :D_MODEL] * jax.nn.sigmoid(z[:, D_MODEL:])).astype(BF16)
            d["pre"] = DN_ALPHA * x_ref[rows, :] + jnp.dot(m, wo_ref[...],
                                                           preferred_element_type=F32)
        else:
            _mlp_stages(d, k - 2, rows, g1_ref, b1_ref, w1_ref, w2_ref, g2_ref, b2_ref, o_ref)

    _skewed(MLP_CHUNKS + 4, stage)


def _attn_out_mlp_kernel(a_ref, r_ref, wo_ref, g1_ref, b1_ref, w1_ref, w2_ref, g2_ref, b2_ref, o_ref):
    state = [dict() for _ in range(ROW_SUBTILES)]

    def stage(r, k):
        d, rows = state[r], _sub_rows(o_ref, r)
        if k == 0:
            d["pre"] = DN_ALPHA * r_ref[rows, :] + jnp.dot(a_ref[rows, :], wo_ref[...],
                                                           preferred_element_type=F32)
        else:
            _mlp_stages(d, k - 1, rows, g1_ref, b1_ref, w1_ref, w2_ref, g2_ref, b2_ref, o_ref)

    _skewed(MLP_CHUNKS + 3, stage)


def _row_stage(body, name, row_inputs, const_inputs, w1_all, w2_all, layer, ln2, tm=ROW_TILE):
    m = row_inputs[0].shape[0]
    row = lambda i: (i, 0)
    layer_spec = lambda w: pl.BlockSpec((None,) + w.shape[1:], lambda i: (layer, 0, 0),
                                        pipeline_mode=pl.Buffered(1))
    return pl.pallas_call(
        body,
        grid=(m // tm,),
        in_specs=([pl.BlockSpec((tm, a.shape[1]), row) for a in row_inputs]
                  + [_const_spec(c.shape) for c in const_inputs]
                  + [layer_spec(w1_all), layer_spec(w2_all)]
                  + [_const_spec(c.shape) for c in ln2]),
        out_specs=pl.BlockSpec((tm, D_MODEL), row),
        out_shape=jax.ShapeDtypeStruct((m, D_MODEL), F32),
        compiler_params=pltpu.CompilerParams(
            dimension_semantics=("parallel",), vmem_limit_bytes=VMEM_LIMIT_BYTES),
        name=name,
    )(*row_inputs, *const_inputs, w1_all, w2_all, *ln2)


ROPE_HALF = QK_ROPE // 2
PROJ_A_COLS = 3 * MXU_DIM
Q_ROPE_TILES = N_HEADS * QK_ROPE // LANES


def _mla_proj_kernel(h_ref, pos_ref, invf_ref, wa_ref, kvg_ref, kvwb_ref, qg_ref, qwb_ref,
                     q_ref, kn_ref, kr_ref, v_ref):
    state = [dict() for _ in range(ROW_SUBTILES)]
    c_q0 = KV_LORA
    k_rope0 = KV_LORA + Q_LORA

    def rope(x, d):
        swapped = jnp.where(d["first"], pltpu.roll(x, LANES - ROPE_HALF, axis=1),
                            pltpu.roll(x, ROPE_HALF, axis=1))
        return x * d["cos"] + swapped * d["sin"]

    def stage(r, k):
        d, rows = state[r], _sub_rows(h_ref, r)
        if k == 0:
            d["a"] = jnp.dot(h_ref[rows, :].astype(BF16), wa_ref[...], preferred_element_type=F32)
            ang = pos_ref[rows, :] * invf_ref[...]
            lane = lax.broadcasted_iota(jnp.int32, ang.shape, 1)
            d["first"] = (lane & ROPE_HALF) == 0
            d["low"] = lane < QK_ROPE
            d["cos"] = jnp.cos(ang)
            d["sin"] = jnp.sin(ang)
        elif k == 1:
            a = d.pop("a")
            kr_ref[rows, :] = rope(a[:, k_rope0:k_rope0 + LANES], d).astype(kr_ref.dtype)
            c_kv = _rms_norm(a[:, :KV_LORA], kvg_ref[...]).astype(BF16)
            d["c_q"] = _rms_norm(a[:, c_q0:c_q0 + Q_LORA], qg_ref[...]).astype(BF16)
            d["kv"] = jnp.dot(c_kv, kvwb_ref[...], preferred_element_type=F32)
        elif k == 2:
            kv = d.pop("kv")
            for hd in range(N_HEADS):
                kn_ref[hd, rows, :] = kv[:, hd * QK_NOPE:(hd + 1) * QK_NOPE].astype(kn_ref.dtype)
                v0 = N_HEADS * QK_NOPE + hd * V_HEAD
                v_ref[hd, rows, :] = kv[:, v0:v0 + V_HEAD].astype(v_ref.dtype)
            d["q"] = jnp.dot(d.pop("c_q"), qwb_ref[...], preferred_element_type=F32) * SM_SCALE
        else:
            q = d.pop("q")
            for hd in range(N_HEADS):
                q_ref[hd, rows, :LANES] = q[:, hd * LANES:(hd + 1) * LANES].astype(q_ref.dtype)
            for p in range(Q_ROPE_TILES):
                c0 = N_HEADS * QK_NOPE + p * LANES
                y = rope(q[:, c0:c0 + LANES], d)
                for hd, y_hd in ((2 * p, y), (2 * p + 1, pltpu.roll(y, QK_ROPE, axis=1))):
                    q_ref[hd, rows, LANES:] = jnp.where(d["low"], y_hd, 0.0).astype(q_ref.dtype)

    _skewed(4, stage)


def _mla_proj(h, pos, inv_freq, w_a, kv_g, kv_wb, q_g, q_wb, nb, seq, tm=ROW_TILE):
    nt = seq // tm
    row = lambda b, i: (b * nt + i, 0)
    head = lambda b, i: (b, 0, i, 0)
    consts = (inv_freq, w_a, kv_g, kv_wb, q_g, q_wb)
    return pl.pallas_call(
        _mla_proj_kernel,
        grid=(nb, nt),
        in_specs=[pl.BlockSpec((tm, D_MODEL), row), pl.BlockSpec((tm, 1), row)]
        + [_const_spec(c.shape) for c in consts],
        out_specs=[pl.BlockSpec((None, N_HEADS, tm, 2 * LANES), head),
                   pl.BlockSpec((None, N_HEADS, tm, QK_NOPE), head),
                   pl.BlockSpec((None, tm, LANES), lambda b, i: (b, i, 0)),
                   pl.BlockSpec((None, N_HEADS, tm, V_HEAD), head)],
        out_shape=[jax.ShapeDtypeStruct((nb, N_HEADS, seq, 2 * LANES), BF16),
                   jax.ShapeDtypeStruct((nb, N_HEADS, seq, QK_NOPE), BF16),
                   jax.ShapeDtypeStruct((nb, seq, LANES), BF16),
                   jax.ShapeDtypeStruct((nb, N_HEADS, seq, V_HEAD), BF16)],
        compiler_params=pltpu.CompilerParams(
            dimension_semantics=("parallel", "parallel"), vmem_limit_bytes=VMEM_LIMIT_BYTES),
        name="mla_proj",
    )(h, pos, *consts)


def _mla_weights(kv_w_a, kv_w_b, q_w_a, q_w_b):
    pad = jnp.zeros((D_MODEL, PROJ_A_COLS - KV_LORA - Q_LORA - QK_ROPE), F32)
    w_a = jnp.concatenate([kv_w_a[:, :KV_LORA], q_w_a, kv_w_a[:, KV_LORA:], pad], axis=1)
    kv_wb3 = kv_w_b.reshape(KV_LORA, N_HEADS, QK_NOPE + V_HEAD)
    kv_wb = jnp.concatenate([kv_wb3[:, :, :QK_NOPE].reshape(KV_LORA, -1),
                             kv_wb3[:, :, QK_NOPE:].reshape(KV_LORA, -1)], axis=1)
    q_wb3 = q_w_b.reshape(Q_LORA, N_HEADS, QK_NOPE + QK_ROPE)
    q_wb = jnp.concatenate([q_wb3[:, :, :QK_NOPE].reshape(Q_LORA, -1),
                            q_wb3[:, :, QK_NOPE:].reshape(Q_LORA, -1)], axis=1)
    return w_a.astype(BF16), kv_wb.astype(BF16), q_wb.astype(BF16)


def _attn_kernel(q_ref, kn_ref, kr_ref, v_ref, o_ref, s_ref, m_ref, acc_ref):
    tq = tk = ATTN_TQ
    half = tq // 2
    n_q = q_ref.shape[0] // tq
    blocks = [(qi, j) for qi in range(n_q) for j in range(qi + 1)]
    nt = (((1,), (1,)), ((), ()))
    ones = jnp.ones((tk, LANES), BF16)

    def pieces(qi, j):
        return ((0, half, half), (half, tq, tk)) if j == qi else ((0, tq, tk),)

    def scores(n):
        qi, j = blocks[n]
        k = jnp.concatenate([kn_ref[j * tk:(j + 1) * tk, :], kr_ref[j * tk:(j + 1) * tk, :]], axis=1)
        for r0, r1, nk in pieces(qi, j):
            s_ref[n % 2, r0:r1, :nk] = lax.dot_general(
                q_ref[qi * tq + r0:qi * tq + r1, :], k[:nk], nt, preferred_element_type=F32)

    def consume(n):
        qi, j = blocks[n]
        buf = qi % 2
        for r0, r1, nk in pieces(qi, j):
            s = s_ref[n % 2, r0:r1, :nk]
            if j == qi:
                sq = s[:, nk - half:]
                keep = (lax.broadcasted_iota(jnp.int32, sq.shape, 1)
                        <= lax.broadcasted_iota(jnp.int32, sq.shape, 0))
                sq = jnp.where(keep, sq, NEG_INF)
                s = sq if nk == half else jnp.concatenate([s[:, :nk - half], sq], axis=1)
            m_new = jnp.max(s, axis=-1, keepdims=True)
            if j == 0:
                m_new = jnp.broadcast_to(m_new, (r1 - r0, LANES))
            else:
                m_old = m_ref[buf, r0:r1, :]
                m_new = jnp.maximum(m_old, m_new)
            p = jnp.exp(s - jnp.concatenate([m_new] * (nk // LANES), axis=1)).astype(BF16)
            v_ext = jnp.concatenate([v_ref[j * tk:j * tk + nk, :], ones[:nk]], axis=1)
            acc = jnp.dot(p, v_ext, preferred_element_type=F32)
            if j > 0:
                alpha = jnp.exp(m_old - m_new)
                acc = jnp.concatenate([alpha, alpha], axis=1) * acc_ref[buf, r0:r1, :] + acc
            if j == qi:
                o_ref[qi * tq + r0:qi * tq + r1, :] = (
                    acc[:, :V_HEAD] / acc[:, V_HEAD:]).astype(o_ref.dtype)
            else:
                acc_ref[buf, r0:r1, :] = acc
                m_ref[buf, r0:r1, :] = m_new

    scores(0)
    for n in range(len(blocks)):
        if n + 1 < len(blocks):
            scores(n + 1)
        consume(n)


def _attention(q, kn, kr, v):
    nb, nh, seq, _ = q.shape
    assert seq % ATTN_TQ == 0
    return pl.pallas_call(
        _attn_kernel,
        grid=(nb, nh),
        in_specs=[pl.BlockSpec((None, None, seq, 2 * LANES), lambda b, h: (b, h, 0, 0)),
                  pl.BlockSpec((None, None, seq, QK_NOPE), lambda b, h: (b, h, 0, 0)),
                  pl.BlockSpec((None, seq, LANES), lambda b, h: (b, 0, 0)),
                  pl.BlockSpec((None, None, seq, V_HEAD), lambda b, h: (b, h, 0, 0))],
        out_specs=pl.BlockSpec((None, seq, V_HEAD), lambda b, h: (b, 0, h)),
        out_shape=jax.ShapeDtypeStruct((nb, seq, nh * V_HEAD), BF16),
        scratch_shapes=[pltpu.VMEM((2, ATTN_TQ, ATTN_TQ), F32),
                        pltpu.VMEM((2, ATTN_TQ, LANES), F32),
                        pltpu.VMEM((2, ATTN_TQ, V_HEAD + LANES), F32)],
        compiler_params=pltpu.CompilerParams(
            dimension_semantics=("parallel", "parallel"), vmem_limit_bytes=VMEM_LIMIT_BYTES),
        name="attention",
    )(q, kn, kr, v)


def _rope_inv_freq():
    inv_freq = ROPE_THETA ** (-jnp.arange(ROPE_HALF, dtype=F32) / ROPE_HALF)
    group = jnp.concatenate([-inv_freq, inv_freq])
    return jnp.tile(group, LANES // QK_ROPE).reshape(1, LANES)


def _row(v):
    return v.reshape(1, -1).astype(F32)


def kernel(x, positions, ln_mix_g, ln_mix_b, ln_ffn_g, ln_ffn_b, w_ff1, w_ff2, ssm_lam_re, ssm_lam_im, ssm_log_dt, ssm_b_re, ssm_b_im, ssm_c_re, ssm_c_im, ssm_d, ssm_w_glu, ssm_w_out, kv_w_a, kv_norm_g, kv_w_b, q_w_a, q_norm_g, q_w_b, attn_w_o):
    nb, seq, _ = x.shape
    m = nb * seq
    pos = positions.astype(F32).reshape(m, 1)
    inv_freq = _rope_inv_freq()
    w1_all = w_ff1.astype(BF16)
    w2_all = w_ff2.astype(BF16)

    h = x.reshape(m, D_MODEL)
    kn = kr = v = None
    for layer in range(DEPTH):
        ln1 = (_row(ln_mix_g[layer]), _row(ln_mix_b[layer]))
        ln2 = (_row(ln_ffn_g[layer]), _row(ln_ffn_b[layer]))
        if layer < N_A_LAYERS:
            i = layer
            s5_params = _s5_params(ssm_lam_re[i], ssm_lam_im[i], ssm_log_dt[i],
                                   ssm_b_re[i], ssm_b_im[i], ssm_c_re[i], ssm_c_im[i])
            y = _s5_scan(h.reshape(nb, seq, D_MODEL), s5_params, _row(ssm_d[i]))
            h = _row_stage(_s5_tail_mlp_kernel, "s5_tail_mlp", (y.reshape(m, D_MODEL), h),
                           (ssm_w_glu[i].astype(BF16), ssm_w_out[i].astype(BF16)) + ln1,
                           w1_all, w2_all, layer, ln2)
        else:
            j = layer - N_A_LAYERS
            w_a, kv_wb, q_wb = _mla_weights(kv_w_a, kv_w_b, q_w_a[j], q_w_b[j])
            q, kn_j, kr_j, v_j = _mla_proj(h, pos, inv_freq, w_a, _row(kv_norm_g), kv_wb,
                                           _row(q_norm_g[j]), q_wb, nb, seq)
            if layer == N_A_LAYERS:
                kn, kr, v = kn_j, kr_j, v_j
            o = _attention(q, kn, kr, v)
            h = _row_stage(_attn_out_mlp_kernel, "attn_out_mlp", (o.reshape(m, N_HEADS * V_HEAD), h),
                           (attn_w_o[j].astype(BF16),) + ln1, w1_all, w2_all, layer, ln2)
    return h.reshape(nb, seq, D_MODEL)
```

```python
import math

import jax
import numpy as np
import jax.numpy as jnp
from jax import lax
from jax.experimental import pallas as pl
from jax.experimental.pallas import tpu as pltpu

F32 = jnp.float32
BF16 = jnp.bfloat16

D_MODEL = 1024
DEPTH = 2
SSM_GROUP = 16
N_GROUPS = D_MODEL // SSM_GROUP
SSM_STATE = 64
N_HEADS = 8
QK_NOPE = 128
QK_ROPE = 64
V_HEAD = 128
Q_LORA = 384
KV_LORA = 256
ROPE_THETA = 10000.0
SM_SCALE = (QK_NOPE + QK_ROPE) ** -0.5
NEG_INF = -1e30
D_FF = 4 * D_MODEL
N_A_LAYERS = DEPTH // 2
DN_ALPHA = (2 * DEPTH) ** 0.25
LN_EPS = 1e-5
RMS_EPS = 1e-6

LANES = 128
SUBLANES = 8
MXU_DIM = 256
VMEM_LIMIT_BYTES = 56 * 1024 * 1024

S5_L = 8
S5_T = 256
S5_PAIRS = N_GROUPS // 2
S5_N_SLABS = D_MODEL // LANES
S5_SUPER = 4 * LANES
S5_PITCH = S5_T + SUBLANES

ROW_TILE = 512
ATTN_TQ = 512


def _layer_norm(x, g, b):
    mu = jnp.mean(x, axis=-1, keepdims=True)
    xc = x - mu
    var = jnp.mean(xc * xc, axis=-1, keepdims=True)
    return xc * lax.rsqrt(var + LN_EPS) * g + b


def _rms_norm(x, g):
    return x * lax.rsqrt(jnp.mean(x * x, axis=-1, keepdims=True) + RMS_EPS) * g


def _gelu_tanh(x):
    c = math.sqrt(2.0 / math.pi)
    return 0.5 * x * (1.0 + jnp.tanh(c * (x + 0.044715 * (x * x * x))))


def _const_spec(shape):
    nd = len(shape)
    return pl.BlockSpec(shape, lambda *_: (0,) * nd, pipeline_mode=pl.Buffered(1))


def _s5_kernel(x_ref, perm_ref, permt_ref, wi_ref, wt_ref, wo_ref, a8r_ref, a8i_ref, d_ref, o_ref,
               xs_ref, hc_ref):
    nb, t_len, _ = x_ref.shape
    n_chunks = t_len // S5_L
    halves = S5_L * LANES // S5_SUPER

    @pl.when(pl.program_id(0) == 0)
    def _():
        hc_ref[...] = jnp.zeros_like(hc_ref)

    for b in range(nb):
        x_b = x_ref[b]
        for l in range(S5_N_SLABS):
            xs_ref[l, pl.ds(b * S5_PITCH, t_len), :] = x_b[:, l * LANES:(l + 1) * LANES]

    def step_rows(c, j):
        return pl.ds(S5_L * c + j, nb, stride=S5_PITCH)

    for l in range(S5_N_SLABS):
        d_l = d_ref[:, l * LANES:(l + 1) * LANES]
        z_half = []
        for a in range(halves):
            lhs = jnp.concatenate(
                [jnp.concatenate([xs_ref[l, step_rows(c, 4 * a + jj), :] for c in range(n_chunks)],
                                 axis=0) for jj in range(4)], axis=1).astype(BF16)
            z_half.append(jnp.dot(lhs, perm_ref[...], preferred_element_type=F32).astype(BF16))
        y_pair = []
        for gh in range(4):
            q = 4 * l + gh
            z_q = jnp.concatenate([zh[:, gh * LANES:(gh + 1) * LANES] for zh in z_half], axis=1)
            w_in = jnp.concatenate([wi_ref[j, q] for j in range(S5_L)], axis=0)
            w_y = jnp.concatenate([wt_ref[j, q] for j in range(S5_L)] + [wo_ref[0, q], wo_ref[1, q]],
                                  axis=0)
            s_q = jnp.dot(z_q, w_in, preferred_element_type=F32)
            a_re, a_im = a8r_ref[q], a8i_ref[q]
            h_re, h_im = hc_ref[2 * q], hc_ref[2 * q + 1]
            ent_re, ent_im = [], []
            for c in range(n_chunks):
                ent_re.append(h_re)
                ent_im.append(h_im)
                rows_c = slice(c * nb, (c + 1) * nb)
                n_re = a_re * h_re - a_im * h_im + s_q[rows_c, :LANES]
                n_im = a_re * h_im + a_im * h_re + s_q[rows_c, LANES:]
                h_re, h_im = n_re, n_im
            hc_ref[2 * q] = h_re
            hc_ref[2 * q + 1] = h_im
            ent = jnp.concatenate([jnp.concatenate(ent_re, axis=0),
                                   jnp.concatenate(ent_im, axis=0)], axis=1).astype(BF16)
            y_pair.append(jnp.dot(jnp.concatenate([z_q, ent], axis=1), w_y,
                                  preferred_element_type=F32))
        for a in range(halves):
            y_l = jnp.concatenate([y[:, a * LANES:(a + 1) * LANES] for y in y_pair], axis=1)
            out = jnp.dot(y_l.astype(BF16), permt_ref[...], preferred_element_type=F32)
            for jj in range(4):
                for c in range(n_chunks):
                    rows = step_rows(c, 4 * a + jj)
                    xs_ref[l, rows, :] = (out[c * nb:(c + 1) * nb, jj * LANES:(jj + 1) * LANES]
                                          + d_l * xs_ref[l, rows, :])
        for b in range(nb):
            o_ref[b, :, l * LANES:(l + 1) * LANES] = _gelu_tanh(
                xs_ref[l, pl.ds(b * S5_PITCH, t_len), :]).astype(o_ref.dtype)


def _s5_scan(x, params, d_skip, t_len=S5_T):
    nb, seq, _ = x.shape
    assert nb == SUBLANES and seq % t_len == 0 and t_len % S5_L == 0
    return pl.pallas_call(
        _s5_kernel,
        grid=(seq // t_len,),
        in_specs=[pl.BlockSpec((nb, t_len, D_MODEL), lambda i: (0, i, 0))]
        + [_const_spec(p.shape) for p in params] + [_const_spec(d_skip.shape)],
        out_specs=pl.BlockSpec((nb, t_len, D_MODEL), lambda i: (0, i, 0)),
        out_shape=jax.ShapeDtypeStruct(x.shape, BF16),
        scratch_shapes=[
            pltpu.VMEM((S5_N_SLABS, nb * S5_PITCH, LANES), F32),
            pltpu.VMEM((N_GROUPS, SUBLANES, LANES), F32),
        ],
        compiler_params=pltpu.CompilerParams(
            dimension_semantics=("arbitrary",), vmem_limit_bytes=VMEM_LIMIT_BYTES),
        name="s5_scan",
    )(x, *params, d_skip)


def _s5_params(lam_re, lam_im, log_dt, b_re, b_im, c_re, c_im):
    dt = jnp.exp(log_dt)[:, None]
    mag = jnp.exp(lam_re * dt)
    a_re = mag * jnp.cos(lam_im * dt)
    a_im = mag * jnp.sin(lam_im * dt)
    inv_den = 1.0 / (lam_re * lam_re + lam_im * lam_im)
    coef_re = ((a_re - 1.0) * lam_re + a_im * lam_im) * inv_den
    coef_im = (a_im * lam_re - (a_re - 1.0) * lam_im) * inv_den
    bb_re = coef_re[..., None] * b_re - coef_im[..., None] * b_im
    bb_im = coef_re[..., None] * b_im + coef_im[..., None] * b_re
    pw_re, pw_im = [jnp.ones_like(a_re)], [jnp.zeros_like(a_im)]
    for _ in range(S5_L):
        pw_re, pw_im = (pw_re + [pw_re[-1] * a_re - pw_im[-1] * a_im],
                        pw_im + [pw_re[-1] * a_im + pw_im[-1] * a_re])
    pw_re, pw_im = jnp.stack(pw_re), jnp.stack(pw_im)
    ca_re = c_re[None] * pw_re[:, :, None, :] - c_im[None] * pw_im[:, :, None, :]
    ca_im = c_re[None] * pw_im[:, :, None, :] + c_im[None] * pw_re[:, :, None, :]
    eye2 = np.eye(2, dtype=np.float32)
    ch2, st2 = 2 * SSM_GROUP, 2 * SSM_STATE

    kern = (jnp.einsum('kgop,gpi->kgio', ca_re[:S5_L], bb_re)
            - jnp.einsum('kgop,gpi->kgio', ca_im[:S5_L], bb_im))
    kern = kern.astype(BF16).reshape(S5_L, S5_PAIRS, 2, SSM_GROUP, SSM_GROUP)
    by_lag = (kern[:, :, :, :, None, :] * eye2[None, None, :, None, :, None].astype(BF16)
              ).reshape(S5_L, S5_PAIRS, ch2, ch2)
    lag = np.arange(S5_L)[None, :] - np.arange(S5_L)[:, None]
    toep = jnp.where((lag >= 0)[:, :, None, None, None], by_lag[np.clip(lag, 0, None)], 0)
    wt = jnp.transpose(toep, (0, 2, 3, 1, 4)).reshape(S5_L, S5_PAIRS, ch2, S5_L * ch2)

    rev = S5_L - 1 - np.arange(S5_L)
    rev_re, rev_im = pw_re[rev][..., None], pw_im[rev][..., None]
    ab = jnp.stack([rev_re * bb_re[None] - rev_im * bb_im[None],
                    rev_re * bb_im[None] + rev_im * bb_re[None]])
    ab = jnp.transpose(ab, (1, 2, 4, 0, 3)).astype(BF16).reshape(
        S5_L, S5_PAIRS, 2, SSM_GROUP, 2, SSM_STATE)
    wi = (ab[:, :, :, :, :, None, :] * eye2[None, None, :, None, None, :, None].astype(BF16)
          ).reshape(S5_L, S5_PAIRS, ch2, 2 * st2)

    co = jnp.stack([ca_re[1:], -ca_im[1:]])
    co = jnp.transpose(co, (0, 2, 4, 1, 3)).astype(BF16).reshape(
        2, S5_PAIRS, 2, SSM_STATE, S5_L, SSM_GROUP)
    wo = (co[:, :, :, :, :, None, :] * eye2[None, None, :, None, None, :, None].astype(BF16)
          ).reshape(2, S5_PAIRS, st2, S5_L * ch2)

    src = np.arange(S5_SUPER).reshape(4, 4, ch2)
    dst = np.transpose(src, (1, 0, 2)).reshape(-1)
    perm = jnp.asarray(dst[None, :] == np.arange(S5_SUPER)[:, None], BF16)

    def bcast(a):
        return jnp.broadcast_to(a.reshape(S5_PAIRS, 1, LANES), (S5_PAIRS, SUBLANES, LANES))

    return perm, perm.T, wi, wt, wo, bcast(pw_re[S5_L]), bcast(pw_im[S5_L])


ROW_SUBTILES = 2
MLP_CHUNKS = D_FF // D_MODEL


def _skewed(n_stages, stage):
    for step in range(n_stages + ROW_SUBTILES - 1):
        for r in range(ROW_SUBTILES):
            k = step - r
            if 0 <= k < n_stages:
                stage(r, k)


def _sub_rows(ref, r):
    sub = ref.shape[0] // ROW_SUBTILES
    return slice(r * sub, (r + 1) * sub)


def _mlp_stages(d, k, rows, g1_ref, b1_ref, w1_ref, w2_ref, g2_ref, b2_ref, o_ref):
    if k == 0:
        h = _layer_norm(d.pop("pre"), g1_ref[...], b1_ref[...])
        d["hb"] = h.astype(BF16)
        d["acc"] = DN_ALPHA * h
    elif k <= MLP_CHUNKS:
        cols = slice((k - 1) * D_MODEL, k * D_MODEL)
        u = jnp.maximum(jnp.dot(d["hb"], w1_ref[:, cols], preferred_element_type=F32), 0.0)
        d["acc"] = d["acc"] + jnp.dot((u * u).astype(BF16), w2_ref[cols, :],
                                      preferred_element_type=F32)
    else:
        o_ref[rows, :] = _layer_norm(d.pop("acc"), g2_ref[...], b2_ref[...])


def _s5_tail_mlp_kernel(y_ref, x_ref, wg_ref, wo_ref, g1_ref, b1_ref, w1_ref, w2_ref, g2_ref, b2_ref,
                        o_ref):
    state = [dict() for _ in range(ROW_SUBTILES)]

    def stage(r, k):
        d, rows = state[r], _sub_rows(o_ref, r)
        if k == 0:
            d["z"] = jnp.dot(y_ref[rows, :], wg_ref[...], preferred_element_type=F32)
        elif k == 1:
            z = d.pop("z")
            m = (z[:, :D_MODEL] * jax.nn.sigmoid(z[:, D_MODEL:])).astype(BF16)
            d["pre"] = DN_ALPHA * x_ref[rows, :] + jnp.dot(m, wo_ref[...],
                                                           preferred_element_type=F32)
        else:
            _mlp_stages(d, k - 2, rows, g1_ref, b1_ref, w1_ref, w2_ref, g2_ref, b2_ref, o_ref)

    _skewed(MLP_CHUNKS + 4, stage)


def _attn_out_mlp_kernel(a_ref, r_ref, wo_ref, g1_ref, b1_ref, w1_ref, w2_ref, g2_ref, b2_ref, o_ref):
    state = [dict() for _ in range(ROW_SUBTILES)]

    def stage(r, k):
        d, rows = state[r], _sub_rows(o_ref, r)
        if k == 0:
            d["pre"] = DN_ALPHA * r_ref[rows, :] + jnp.dot(a_ref[rows, :], wo_ref[...],
                                                           preferred_element_type=F32)
        else:
            _mlp_stages(d, k - 1, rows, g1_ref, b1_ref, w1_ref, w2_ref, g2_ref, b2_ref, o_ref)

    _skewed(MLP_CHUNKS + 3, stage)


def _row_stage(body, name, row_inputs, const_inputs, w1_all, w2_all, layer, ln2, tm=ROW_TILE):
    m = row_inputs[0].shape[0]
    row = lambda i: (i, 0)
    layer_spec = lambda w: pl.BlockSpec((None,) + w.shape[1:], lambda i: (layer, 0, 0),
                                        pipeline_mode=pl.Buffered(1))
    return pl.pallas_call(
        body,
        grid=(m // tm,),
        in_specs=([pl.BlockSpec((tm, a.shape[1]), row) for a in row_inputs]
                  + [_const_spec(c.shape) for c in const_inputs]
                  + [layer_spec(w1_all), layer_spec(w2_all)]
                  + [_const_spec(c.shape) for c in ln2]),
        out_specs=pl.BlockSpec((tm, D_MODEL), row),
        out_shape=jax.ShapeDtypeStruct((m, D_MODEL), F32),
        compiler_params=pltpu.CompilerParams(
            dimension_semantics=("parallel",), vmem_limit_bytes=VMEM_LIMIT_BYTES),
        name=name,
    )(*row_inputs, *const_inputs, w1_all, w2_all, *ln2)


ROPE_HALF = QK_ROPE // 2
PROJ_A_COLS = 3 * MXU_DIM
Q_ROPE_TILES = N_HEADS * QK_ROPE // LANES


def _mla_proj_kernel(h_ref, pos_ref, invf_ref, wa_ref, kvg_ref, kvwb_ref, qg_ref, qwb_ref,
                     q_ref, kn_ref, kr_ref, v_ref):
    state = [dict() for _ in range(ROW_SUBTILES)]
    c_q0 = KV_LORA
    k_rope0 = KV_LORA + Q_LORA

    def rope(x, d):
        swapped = jnp.where(d["first"], pltpu.roll(x, LANES - ROPE_HALF, axis=1),
                            pltpu.roll(x, ROPE_HALF, axis=1))
        return x * d["cos"] + swapped * d["sin"]

    def stage(r, k):
        d, rows = state[r], _sub_rows(h_ref, r)
        if k == 0:
            d["a"] = jnp.dot(h_ref[rows, :].astype(BF16), wa_ref[...], preferred_element_type=F32)
            ang = pos_ref[rows, :] * invf_ref[...]
            lane = lax.broadcasted_iota(jnp.int32, ang.shape, 1)
            d["first"] = (lane & ROPE_HALF) == 0
            d["low"] = lane < QK_ROPE
            d["cos"] = jnp.cos(ang)
            d["sin"] = jnp.sin(ang)
        elif k == 1:
            a = d.pop("a")
            kr_ref[rows, :] = rope(a[:, k_rope0:k_rope0 + LANES], d).astype(kr_ref.dtype)
            c_kv = _rms_norm(a[:, :KV_LORA], kvg_ref[...]).astype(BF16)
            d["c_q"] = _rms_norm(a[:, c_q0:c_q0 + Q_LORA], qg_ref[...]).astype(BF16)
            d["kv"] = jnp.dot(c_kv, kvwb_ref[...], preferred_element_type=F32)
        elif k == 2:
            kv = d.pop("kv")
            for hd in range(N_HEADS):
                kn_ref[hd, rows, :] = kv[:, hd * QK_NOPE:(hd + 1) * QK_NOPE].astype(kn_ref.dtype)
                v0 = N_HEADS * QK_NOPE + hd * V_HEAD
                v_ref[hd, rows, :] = kv[:, v0:v0 + V_HEAD].astype(v_ref.dtype)
            d["q"] = jnp.dot(d.pop("c_q"), qwb_ref[...], preferred_element_type=F32) * SM_SCALE
        else:
            q = d.pop("q")
            for hd in range(N_HEADS):
                q_ref[hd, rows, :LANES] = q[:, hd * LANES:(hd + 1) * LANES].astype(q_ref.dtype)
            for p in range(Q_ROPE_TILES):
                c0 = N_HEADS * QK_NOPE + p * LANES
                y = rope(q[:, c0:c0 + LANES], d)
                for hd, y_hd in ((2 * p, y), (2 * p + 1, pltpu.roll(y, QK_ROPE, axis=1))):
                    q_ref[hd, rows, LANES:] = jnp.where(d["low"], y_hd, 0.0).astype(q_ref.dtype)

    _skewed(4, stage)


def _mla_proj(h, pos, inv_freq, w_a, kv_g, kv_wb, q_g, q_wb, nb, seq, tm=ROW_TILE):
    nt = seq // tm
    row = lambda b, i: (b * nt + i, 0)
    head = lambda b, i: (b, 0, i, 0)
    consts = (inv_freq, w_a, kv_g, kv_wb, q_g, q_wb)
    return pl.pallas_call(
        _mla_proj_kernel,
        grid=(nb, nt),
        in_specs=[pl.BlockSpec((tm, D_MODEL), row), pl.BlockSpec((tm, 1), row)]
        + [_const_spec(c.shape) for c in consts],
        out_specs=[pl.BlockSpec((None, N_HEADS, tm, 2 * LANES), head),
                   pl.BlockSpec((None, N_HEADS, tm, QK_NOPE), head),
                   pl.BlockSpec((None, tm, LANES), lambda b, i: (b, i, 0)),
                   pl.BlockSpec((None, N_HEADS, tm, V_HEAD), head)],
        out_shape=[jax.ShapeDtypeStruct((nb, N_HEADS, seq, 2 * LANES), BF16),
                   jax.ShapeDtypeStruct((nb, N_HEADS, seq, QK_NOPE), BF16),
                   jax.ShapeDtypeStruct((nb, seq, LANES), BF16),
                   jax.ShapeDtypeStruct((nb, N_HEADS, seq, V_HEAD), BF16)],
        compiler_params=pltpu.CompilerParams(
            dimension_semantics=("parallel", "parallel"), vmem_limit_bytes=VMEM_LIMIT_BYTES),
        name="mla_proj",
    )(h, pos, *consts)


def _mla_weights(kv_w_a, kv_w_b, q_w_a, q_w_b):
    pad = jnp.zeros((D_MODEL, PROJ_A_COLS - KV_LORA - Q_LORA - QK_ROPE), F32)
    w_a = jnp.concatenate([kv_w_a[:, :KV_LORA], q_w_a, kv_w_a[:, KV_LORA:], pad], axis=1)
    kv_wb3 = kv_w_b.reshape(KV_LORA, N_HEADS, QK_NOPE + V_HEAD)
    kv_wb = jnp.concatenate([kv_wb3[:, :, :QK_NOPE].reshape(KV_LORA, -1),
                             kv_wb3[:, :, QK_NOPE:].reshape(KV_LORA, -1)], axis=1)
    q_wb3 = q_w_b.reshape(Q_LORA, N_HEADS, QK_NOPE + QK_ROPE)
    q_wb = jnp.concatenate([q_wb3[:, :, :QK_NOPE].reshape(Q_LORA, -1),
                            q_wb3[:, :, QK_NOPE:].reshape(Q_LORA, -1)], axis=1)
    return w_a.astype(BF16), kv_wb.astype(BF16), q_wb.astype(BF16)


def _attn_kernel(q_ref, kn_ref, kr_ref, v_ref, o_ref, s_ref, m_ref, acc_ref):
    tq = tk = ATTN_TQ
    half = tq // 2
    n_q = q_ref.shape[0] // tq
    blocks = [(qi, j) for qi in range(n_q) for j in range(qi + 1)]
    nt = (((1,), (1,)), ((), ()))
    ones = jnp.ones((tk, LANES), BF16)

    def pieces(qi, j):
        return ((0, half, half), (half, tq, tk)) if j == qi else ((0, tq, tk),)

    def scores(n):
        qi, j = blocks[n]
        k = jnp.concatenate([kn_ref[j * tk:(j + 1) * tk, :], kr_ref[j * tk:(j + 1) * tk, :]], axis=1)
        for r0, r1, nk in pieces(qi, j):
            s_ref[n % 2, r0:r1, :nk] = lax.dot_general(
                q_ref[qi * tq + r0:qi * tq + r1, :], k[:nk], nt, preferred_element_type=F32)

    def consume(n):
        qi, j = blocks[n]
        buf = qi % 2
        for r0, r1, nk in pieces(qi, j):
            s = s_ref[n % 2, r0:r1, :nk]
            if j == qi:
                sq = s[:, nk - half:]
                keep = (lax.broadcasted_iota(jnp.int32, sq.shape, 1)
                        <= lax.broadcasted_iota(jnp.int32, sq.shape, 0))
                sq = jnp.where(keep, sq, NEG_INF)
                s = sq if nk == half else jnp.concatenate([s[:, :nk - half], sq], axis=1)
            m_new = jnp.max(s, axis=-1, keepdims=True)
            if j == 0:
                m_new = jnp.broadcast_to(m_new, (r1 - r0, LANES))
            else:
                m_old = m_ref[buf, r0:r1, :]
                m_new = jnp.maximum(m_old, m_new)
            p = jnp.exp(s - jnp.concatenate([m_new] * (nk // LANES), axis=1)).astype(BF16)
            v_ext = jnp.concatenate([v_ref[j * tk:j * tk + nk, :], ones[:nk]], axis=1)
            acc = jnp.dot(p, v_ext, preferred_element_type=F32)
            if j > 0:
                alpha = jnp.exp(m_old - m_new)
                acc = jnp.concatenate([alpha, alpha], axis=1) * acc_ref[buf, r0:r1, :] + acc
            if j == qi:
                o_ref[qi * tq + r0:qi * tq + r1, :] = (
                    acc[:, :V_HEAD] / acc[:, V_HEAD:]).astype(o_ref.dtype)
            else:
                acc_ref[buf, r0:r1, :] = acc
                m_ref[buf, r0:r1, :] = m_new

    scores(0)
    for n in range(len(blocks)):
        if n + 1 < len(blocks):
            scores(n + 1)
        consume(n)


def _attention(q, kn, kr, v):
    nb, nh, seq, _ = q.shape
    assert seq % ATTN_TQ == 0
    return pl.pallas_call(
        _attn_kernel,
        grid=(nb, nh),
        in_specs=[pl.BlockSpec((None, None, seq, 2 * LANES), lambda b, h: (b, h, 0, 0)),
                  pl.BlockSpec((None, None, seq, QK_NOPE), lambda b, h: (b, h, 0, 0)),
                  pl.BlockSpec((None, seq, LANES), lambda b, h: (b, 0, 0)),
                  pl.BlockSpec((None, None, seq, V_HEAD), lambda b, h: (b, h, 0, 0))],
        out_specs=pl.BlockSpec((None, seq, V_HEAD), lambda b, h: (b, 0, h)),
        out_shape=jax.ShapeDtypeStruct((nb, seq, nh * V_HEAD), BF16),
        scratch_shapes=[pltpu.VMEM((2, ATTN_TQ, ATTN_TQ), F32),
                        pltpu.VMEM((2, ATTN_TQ, LANES), F32),
                        pltpu.VMEM((2, ATTN_TQ, V_HEAD + LANES), F32)],
        compiler_params=pltpu.CompilerParams(
            dimension_semantics=("parallel", "parallel"), vmem_limit_bytes=VMEM_LIMIT_BYTES),
        name="attention",
    )(q, kn, kr, v)


def _rope_inv_freq():
    inv_freq = ROPE_THETA ** (-jnp.arange(ROPE_HALF, dtype=F32) / ROPE_HALF)
    group = jnp.concatenate([-inv_freq, inv_freq])
    return jnp.tile(group, LANES // QK_ROPE).reshape(1, LANES)


def _row(v):
    return v.reshape(1, -1).astype(F32)


def kernel(x, positions, ln_mix_g, ln_mix_b, ln_ffn_g, ln_ffn_b, w_ff1, w_ff2, ssm_lam_re, ssm_lam_im, ssm_log_dt, ssm_b_re, ssm_b_im, ssm_c_re, ssm_c_im, ssm_d, ssm_w_glu, ssm_w_out, kv_w_a, kv_norm_g, kv_w_b, q_w_a, q_norm_g, q_w_b, attn_w_o):
    nb, seq, _ = x.shape
    m = nb * seq
    pos = positions.astype(F32).reshape(m, 1)
    inv_freq = _rope_inv_freq()
    w1_all = w_ff1.astype(BF16)
    w2_all = w_ff2.astype(BF16)

    h = x.reshape(m, D_MODEL)
    kn = kr = v = None
    for layer in range(DEPTH):
        ln1 = (_row(ln_mix_g[layer]), _row(ln_mix_b[layer]))
        ln2 = (_row(ln_ffn_g[layer]), _row(ln_ffn_b[layer]))
        if layer < N_A_LAYERS:
            i = layer
            s5_params = _s5_params(ssm_lam_re[i], ssm_lam_im[i], ssm_log_dt[i],
                                   ssm_b_re[i], ssm_b_im[i], ssm_c_re[i], ssm_c_im[i])
            y = _s5_scan(h.reshape(nb, seq, D_MODEL), s5_params, _row(ssm_d[i]))
            h = _row_stage(_s5_tail_mlp_kernel, "s5_tail_mlp", (y.reshape(m, D_MODEL), h),
                           (ssm_w_glu[i].astype(BF16), ssm_w_out[i].astype(BF16)) + ln1,
                           w1_all, w2_all, layer, ln2)
        else:
            j = layer - N_A_LAYERS
            w_a, kv_wb, q_wb = _mla_weights(kv_w_a, kv_w_b, q_w_a[j], q_w_b[j])
            q, kn_j, kr_j, v_j = _mla_proj(h, pos, inv_freq, w_a, _row(kv_norm_g), kv_wb,
                                           _row(q_norm_g[j]), q_wb, nb, seq)
            if layer == N_A_LAYERS:
                kn, kr, v = kn_j, kr_j, v_j
            o = _attention(q, kn, kr, v)
            h = _row_stage(_attn_out_mlp_kernel, "attn_out_mlp", (o.reshape(m, N_HEADS * V_HEAD), h),
                           (attn_w_o[j].astype(BF16),) + ln1, w1_all, w2_all, layer, ln2)
    return h.reshape(nb, seq, D_MODEL)
```

```python
import math

import jax
import numpy as np
import jax.numpy as jnp
from jax import lax
from jax.experimental import pallas as pl
from jax.experimental.pallas import tpu as pltpu

F32 = jnp.float32
BF16 = jnp.bfloat16

D_MODEL = 1024
DEPTH = 2
SSM_GROUP = 16
N_GROUPS = D_MODEL // SSM_GROUP
SSM_STATE = 64
N_HEADS = 8
QK_NOPE = 128
QK_ROPE = 64
V_HEAD = 128
Q_LORA = 384
KV_LORA = 256
ROPE_THETA = 10000.0
SM_SCALE = (QK_NOPE + QK_ROPE) ** -0.5
NEG_INF = -1e30
D_FF = 4 * D_MODEL
N_A_LAYERS = DEPTH // 2
DN_ALPHA = (2 * DEPTH) ** 0.25
LN_EPS = 1e-5
RMS_EPS = 1e-6

LANES = 128
SUBLANES = 8
MXU_DIM = 256
VMEM_LIMIT_BYTES = 56 * 1024 * 1024

S5_L = 8
S5_T = 256
S5_PAIRS = N_GROUPS // 2
S5_N_SLABS = D_MODEL // LANES
S5_SUPER = 4 * LANES
S5_PITCH = S5_T + SUBLANES

ROW_TILE = 512
ATTN_TQ = 512


def _layer_norm(x, g, b):
    mu = jnp.mean(x, axis=-1, keepdims=True)
    xc = x - mu
    var = jnp.mean(xc * xc, axis=-1, keepdims=True)
    return xc * lax.rsqrt(var + LN_EPS) * g + b


def _rms_norm(x, g):
    return x * lax.rsqrt(jnp.mean(x * x, axis=-1, keepdims=True) + RMS_EPS) * g


def _gelu_tanh(x):
    c = math.sqrt(2.0 / math.pi)
    return 0.5 * x * (1.0 + jnp.tanh(c * (x + 0.044715 * (x * x * x))))


def _const_spec(shape):
    nd = len(shape)
    return pl.BlockSpec(shape, lambda *_: (0,) * nd, pipeline_mode=pl.Buffered(1))


def _s5_kernel(x_ref, perm_ref, permt_ref, wi_ref, wt_ref, wo_ref, a8r_ref, a8i_ref, d_ref, o_ref,
               xs_ref, hc_ref):
    nb, t_len, _ = x_ref.shape
    n_chunks = t_len // S5_L
    halves = S5_L * LANES // S5_SUPER

    @pl.when(pl.program_id(0) == 0)
    def _():
        hc_ref[...] = jnp.zeros_like(hc_ref)

    for b in range(nb):
        x_b = x_ref[b]
        for l in range(S5_N_SLABS):
            xs_ref[l, pl.ds(b * S5_PITCH, t_len), :] = x_b[:, l * LANES:(l + 1) * LANES]

    def step_rows(c, j):
        return pl.ds(S5_L * c + j, nb, stride=S5_PITCH)

    for l in range(S5_N_SLABS):
        d_l = d_ref[:, l * LANES:(l + 1) * LANES]
        z_half = []
        for a in range(halves):
            lhs = jnp.concatenate(
                [jnp.concatenate([xs_ref[l, step_rows(c, 4 * a + jj), :] for c in range(n_chunks)],
                                 axis=0) for jj in range(4)], axis=1).astype(BF16)
            z_half.append(jnp.dot(lhs, perm_ref[...], preferred_element_type=F32).astype(BF16))
        y_pair = []
        for gh in range(4):
            q = 4 * l + gh
            z_q = jnp.concatenate([zh[:, gh * LANES:(gh + 1) * LANES] for zh in z_half], axis=1)
            w_in = jnp.concatenate([wi_ref[j, q] for j in range(S5_L)], axis=0)
            w_y = jnp.concatenate([wt_ref[j, q] for j in range(S5_L)] + [wo_ref[0, q], wo_ref[1, q]],
                                  axis=0)
            s_q = jnp.dot(z_q, w_in, preferred_element_type=F32)
            a_re, a_im = a8r_ref[q], a8i_ref[q]
            h_re, h_im = hc_ref[2 * q], hc_ref[2 * q + 1]
            ent_re, ent_im = [], []
            for c in range(n_chunks):
                ent_re.append(h_re)
                ent_im.append(h_im)
                rows_c = slice(c * nb, (c + 1) * nb)
                n_re = a_re * h_re - a_im * h_im + s_q[rows_c, :LANES]
                n_im = a_re * h_im + a_im * h_re + s_q[rows_c, LANES:]
                h_re, h_im = n_re, n_im
            hc_ref[2 * q] = h_re
            hc_ref[2 * q + 1] = h_im
            ent = jnp.concatenate([jnp.concatenate(ent_re, axis=0),
                                   jnp.concatenate(ent_im, axis=0)], axis=1).astype(BF16)
            y_pair.append(jnp.dot(jnp.concatenate([z_q, ent], axis=1), w_y,
                                  preferred_element_type=F32))
        for a in range(halves):
            y_l = jnp.concatenate([y[:, a * LANES:(a + 1) * LANES] for y in y_pair], axis=1)
            out = jnp.dot(y_l.astype(BF16), permt_ref[...], preferred_element_type=F32)
            for jj in range(4):
                for c in range(n_chunks):
                    rows = step_rows(c, 4 * a + jj)
                    xs_ref[l, rows, :] = (out[c * nb:(c + 1) * nb, jj * LANES:(jj + 1) * LANES]
                                          + d_l * xs_ref[l, rows, :])
        for b in range(nb):
            o_ref[b, :, l * LANES:(l + 1) * LANES] = _gelu_tanh(
                xs_ref[l, pl.ds(b * S5_PITCH, t_len), :]).astype(o_ref.dtype)


def _s5_scan(x, params, d_skip, t_len=S5_T):
    nb, seq, _ = x.shape
    assert nb == SUBLANES and seq % t_len == 0 and t_len % S5_L == 0
    return pl.pallas_call(
        _s5_kernel,
        grid=(seq // t_len,),
        in_specs=[pl.BlockSpec((nb, t_len, D_MODEL), lambda i: (0, i, 0))]
        + [_const_spec(p.shape) for p in params] + [_const_spec(d_skip.shape)],
        out_specs=pl.BlockSpec((nb, t_len, D_MODEL), lambda i: (0, i, 0)),
        out_shape=jax.ShapeDtypeStruct(x.shape, BF16),
        scratch_shapes=[
            pltpu.VMEM((S5_N_SLABS, nb * S5_PITCH, LANES), F32),
            pltpu.VMEM((N_GROUPS, SUBLANES, LANES), F32),
        ],
        compiler_params=pltpu.CompilerParams(
            dimension_semantics=("arbitrary",), vmem_limit_bytes=VMEM_LIMIT_BYTES),
        name="s5_scan",
    )(x, *params, d_skip)


def _s5_params(lam_re, lam_im, log_dt, b_re, b_im, c_re, c_im):
    dt = jnp.exp(log_dt)[:, None]
    mag = jnp.exp(lam_re * dt)
    a_re = mag * jnp.cos(lam_im * dt)
    a_im = mag * jnp.sin(lam_im * dt)
    inv_den = 1.0 / (lam_re * lam_re + lam_im * lam_im)
    coef_re = ((a_re - 1.0) * lam_re + a_im * lam_im) * inv_den
    coef_im = (a_im * lam_re - (a_re - 1.0) * lam_im) * inv_den
    bb_re = coef_re[..., None] * b_re - coef_im[..., None] * b_im
    bb_im = coef_re[..., None] * b_im + coef_im[..., None] * b_re
    pw_re, pw_im = [jnp.ones_like(a_re)], [jnp.zeros_like(a_im)]
    for _ in range(S5_L):
        pw_re, pw_im = (pw_re + [pw_re[-1] * a_re - pw_im[-1] * a_im],
                        pw_im + [pw_re[-1] * a_im + pw_im[-1] * a_re])
    pw_re, pw_im = jnp.stack(pw_re), jnp.stack(pw_im)
    ca_re = c_re[None] * pw_re[:, :, None, :] - c_im[None] * pw_im[:, :, None, :]
    ca_im = c_re[None] * pw_im[:, :, None, :] + c_im[None] * pw_re[:, :, None, :]
    kern = (jnp.einsum('kgop,gpi->kgio', ca_re[:S5_L], bb_re)
            - jnp.einsum('kgop,gpi->kgio', ca_im[:S5_L], bb_im))
    k_c = jnp.transpose(kern, (1, 2, 0, 3)).astype(BF16).reshape(
        S5_PAIRS, 2, SSM_GROUP, LANES)
    rev = S5_L - 1 - np.arange(S5_L)
    rev_re, rev_im = pw_re[rev][..., None], pw_im[rev][..., None]
    ab = jnp.stack([rev_re * bb_re[None] - rev_im * bb_im[None],
                    rev_re * bb_im[None] + rev_im * bb_re[None]])
    a_c = jnp.transpose(ab, (1, 2, 4, 0, 3)).astype(BF16).reshape(
        S5_L, S5_PAIRS, 2, SSM_GROUP, LANES)
    co = jnp.stack([ca_re[1:], -ca_im[1:]])
    c_c = jnp.transpose(co, (0, 2, 4, 1, 3)).astype(BF16).reshape(
        2, S5_PAIRS, 2, SSM_STATE, LANES)
    wt, wi, wo = _s5_weights(k_c, a_c, c_c)

    src = np.arange(S5_SUPER).reshape(4, 4, 2 * SSM_GROUP)
    dst = np.transpose(src, (1, 0, 2)).reshape(-1)
    perm = jnp.asarray(dst[None, :] == np.arange(S5_SUPER)[:, None], BF16)

    def bcast(a):
        return jnp.broadcast_to(a.reshape(S5_PAIRS, 1, LANES), (S5_PAIRS, SUBLANES, LANES))

    return perm, perm.T, wi, wt, wo, bcast(pw_re[S5_L]), bcast(pw_im[S5_L])


def _s5_placements():
    c, p, n = SSM_GROUP, SSM_STATE, S5_L
    shift = np.zeros((n, 2, n, c, n, 2, c), np.float32)
    place_y = np.zeros((2, n, c, n, 2, c), np.float32)
    place_s = np.zeros((2, 2, p, 2, 2, p), np.float32)
    for g in range(2):
        for k in range(n):
            place_y[g, k, :, k, g, :] = np.eye(c)
            for j in range(k + 1):
                shift[j, g, k - j, :, k, g, :] = np.eye(c)
        for part in range(2):
            place_s[g, part, :, part, g, :] = np.eye(p)
    as_bf16 = lambda m, lead: jnp.asarray(m.reshape(lead + (LANES, 2 * LANES)), BF16)
    return as_bf16(shift, (n, 2)), as_bf16(place_s, (2,)), as_bf16(place_y, (2,))


def _s5_weights_kernel(k_ref, a_ref, c_ref, shift_ref, place_s_ref, place_y_ref, wt_ref, wi_ref, wo_ref):
    n_pairs = k_ref.shape[0]
    for g in range(2):
        k_g = k_ref[:, g].reshape(n_pairs * SSM_GROUP, LANES)
        for j in range(S5_L):
            wt_ref[j, :, g] = jnp.dot(k_g, shift_ref[j, g], preferred_element_type=F32).astype(
                wt_ref.dtype).reshape(n_pairs, SSM_GROUP, 2 * LANES)
            a_jg = a_ref[j, :, g].reshape(n_pairs * SSM_GROUP, LANES)
            wi_ref[j, :, g] = jnp.dot(a_jg, place_s_ref[g], preferred_element_type=F32).astype(
                wi_ref.dtype).reshape(n_pairs, SSM_GROUP, 2 * LANES)
        for part in range(2):
            c_pg = c_ref[part, :, g].reshape(n_pairs * SSM_STATE, LANES)
            wo_ref[part, :, g] = jnp.dot(c_pg, place_y_ref[g], preferred_element_type=F32).astype(
                wo_ref.dtype).reshape(n_pairs, SSM_STATE, 2 * LANES)


def _s5_weights(k_c, a_c, c_c):
    wide = 2 * LANES
    wt, wi, wo = pl.pallas_call(
        _s5_weights_kernel,
        out_shape=[jax.ShapeDtypeStruct((S5_L, S5_PAIRS, 2, SSM_GROUP, wide), BF16),
                   jax.ShapeDtypeStruct((S5_L, S5_PAIRS, 2, SSM_GROUP, wide), BF16),
                   jax.ShapeDtypeStruct((2, S5_PAIRS, 2, SSM_STATE, wide), BF16)],
        compiler_params=pltpu.CompilerParams(vmem_limit_bytes=VMEM_LIMIT_BYTES),
        name="s5_weights",
    )(k_c, a_c, c_c, *_s5_placements())
    return (wt.reshape(S5_L, S5_PAIRS, 2 * SSM_GROUP, wide), wi.reshape(S5_L, S5_PAIRS, 2 * SSM_GROUP, wide),
            wo.reshape(2, S5_PAIRS, 2 * SSM_STATE, wide))


ROW_SUBTILES = 2
MLP_CHUNKS = D_FF // D_MODEL


def _skewed(n_stages, stage):
    for step in range(n_stages + ROW_SUBTILES - 1):
        for r in range(ROW_SUBTILES):
            k = step - r
            if 0 <= k < n_stages:
                stage(r, k)


def _sub_rows(ref, r):
    sub = ref.shape[0] // ROW_SUBTILES
    return slice(r * sub, (r + 1) * sub)


def _mlp_stages(d, k, rows, g1_ref, b1_ref, w1_ref, w2_ref, g2_ref, b2_ref, o_ref):
    if k == 0:
        h = _layer_norm(d.pop("pre"), g1_ref[...], b1_ref[...])
        d["hb"] = h.astype(BF16)
        d["acc"] = DN_ALPHA * h
    elif k <= MLP_CHUNKS:
        cols = slice((k - 1) * D_MODEL, k * D_MODEL)
        u = jnp.maximum(jnp.dot(d["hb"], w1_ref[:, cols], preferred_element_type=F32), 0.0)
        d["acc"] = d["acc"] + jnp.dot((u * u).astype(BF16), w2_ref[cols, :],
                                      preferred_element_type=F32)
    else:
        o_ref[rows, :] = _layer_norm(d.pop("acc"), g2_ref[...], b2_ref[...])


def _s5_tail_mlp_kernel(y_ref, x_ref, wg_ref, wo_ref, g1_ref, b1_ref, w1_ref, w2_ref, g2_ref, b2_ref,
                        o_ref):
    state = [dict() for _ in range(ROW_SUBTILES)]

    def stage(r, k):
        d, rows = state[r], _sub_rows(o_ref, r)
        if k == 0:
            d["z"] = jnp.dot(y_ref[rows, :], wg_ref[...], preferred_element_type=F32)
        elif k == 1:
            z = d.pop("z")
            m = (z[:, :D_MODEL] * jax.nn.sigmoid(z[:, D_MODEL:])).astype(BF16)
            d["pre"] = DN_ALPHA * x_ref[rows, :] + jnp.dot(m, wo_ref[...],
                                                           preferred_element_type=F32)
        else:
            _mlp_stages(d, k - 2, rows, g1_ref, b1_ref, w1_ref, w2_ref, g2_ref, b2_ref, o_ref)

    _skewed(MLP_CHUNKS + 4, stage)


def _attn_out_mlp_kernel(a_ref, r_ref, wo_ref, g1_ref, b1_ref, w1_ref, w2_ref, g2_ref, b2_ref, o_ref):
    state = [dict() for _ in range(ROW_SUBTILES)]

    def stage(r, k):
        d, rows = state[r], _sub_rows(o_ref, r)
        if k == 0:
            d["pre"] = DN_ALPHA * r_ref[rows, :] + jnp.dot(a_ref[rows, :], wo_ref[...],
                                                           preferred_element_type=F32)
        else:
            _mlp_stages(d, k - 1, rows, g1_ref, b1_ref, w1_ref, w2_ref, g2_ref, b2_ref, o_ref)

    _skewed(MLP_CHUNKS + 3, stage)


def _row_stage(body, name, row_inputs, const_inputs, w1_all, w2_all, layer, ln2, tm=ROW_TILE):
    m = row_inputs[0].shape[0]
    row = lambda i: (i, 0)
    layer_spec = lambda w: pl.BlockSpec((None,) + w.shape[1:], lambda i: (layer, 0, 0),
                                        pipeline_mode=pl.Buffered(1))
    return pl.pallas_call(
        body,
        grid=(m // tm,),
        in_specs=([pl.BlockSpec((tm, a.shape[1]), row) for a in row_inputs]
                  + [_const_spec(c.shape) for c in const_inputs]
                  + [layer_spec(w1_all), layer_spec(w2_all)]
                  + [_const_spec(c.shape) for c in ln2]),
        out_specs=pl.BlockSpec((tm, D_MODEL), row),
        out_shape=jax.ShapeDtypeStruct((m, D_MODEL), F32),
        compiler_params=pltpu.CompilerParams(
            dimension_semantics=("parallel",), vmem_limit_bytes=VMEM_LIMIT_BYTES),
        name=name,
    )(*row_inputs, *const_inputs, w1_all, w2_all, *ln2)


ROPE_HALF = QK_ROPE // 2
PROJ_A_COLS = 3 * MXU_DIM
Q_ROPE_TILES = N_HEADS * QK_ROPE // LANES


def _mla_proj_kernel(h_ref, pos_ref, invf_ref, wa_ref, kvg_ref, kvwb_ref, qg_ref, qwb_ref,
                     q_ref, kn_ref, kr_ref, v_ref):
    state = [dict() for _ in range(ROW_SUBTILES)]
    c_q0 = KV_LORA
    k_rope0 = KV_LORA + Q_LORA

    def rope(x, d):
        swapped = jnp.where(d["first"], pltpu.roll(x, LANES - ROPE_HALF, axis=1),
                            pltpu.roll(x, ROPE_HALF, axis=1))
        return x * d["cos"] + swapped * d["sin"]

    def stage(r, k):
        d, rows = state[r], _sub_rows(h_ref, r)
        if k == 0:
            d["a"] = jnp.dot(h_ref[rows, :].astype(BF16), wa_ref[...], preferred_element_type=F32)
            ang = pos_ref[rows, :] * invf_ref[...]
            lane = lax.broadcasted_iota(jnp.int32, ang.shape, 1)
            d["first"] = (lane & ROPE_HALF) == 0
            d["low"] = lane < QK_ROPE
            d["cos"] = jnp.cos(ang)
            d["sin"] = jnp.sin(ang)
        elif k == 1:
            a = d.pop("a")
            kr_ref[rows, :] = rope(a[:, k_rope0:k_rope0 + LANES], d).astype(kr_ref.dtype)
            c_kv = _rms_norm(a[:, :KV_LORA], kvg_ref[...]).astype(BF16)
            d["c_q"] = _rms_norm(a[:, c_q0:c_q0 + Q_LORA], qg_ref[...]).astype(BF16)
            d["kv"] = jnp.dot(c_kv, kvwb_ref[...], preferred_element_type=F32)
        elif k == 2:
            kv = d.pop("kv")
            for hd in range(N_HEADS):
                kn_ref[hd, rows, :] = kv[:, hd * QK_NOPE:(hd + 1) * QK_NOPE].astype(kn_ref.dtype)
                v0 = N_HEADS * QK_NOPE + hd * V_HEAD
                v_ref[hd, rows, :] = kv[:, v0:v0 + V_HEAD].astype(v_ref.dtype)
            d["q"] = jnp.dot(d.pop("c_q"), qwb_ref[...], preferred_element_type=F32) * SM_SCALE
        else:
            q = d.pop("q")
            for hd in range(N_HEADS):
                q_ref[hd, rows, :LANES] = q[:, hd * LANES:(hd + 1) * LANES].astype(q_ref.dtype)
            for p in range(Q_ROPE_TILES):
                c0 = N_HEADS * QK_NOPE + p * LANES
                y = rope(q[:, c0:c0 + LANES], d)
                for hd, y_hd in ((2 * p, y), (2 * p + 1, pltpu.roll(y, QK_ROPE, axis=1))):
                    q_ref[hd, rows, LANES:] = jnp.where(d["low"], y_hd, 0.0).astype(q_ref.dtype)

    _skewed(4, stage)


def _mla_proj(h, pos, inv_freq, w_a, kv_g, kv_wb, q_g, q_wb, nb, seq, tm=ROW_TILE):
    nt = seq // tm
    row = lambda b, i: (b * nt + i, 0)
    head = lambda b, i: (b, 0, i, 0)
    consts = (inv_freq, w_a, kv_g, kv_wb, q_g, q_wb)
    return pl.pallas_call(
        _mla_proj_kernel,
        grid=(nb, nt),
        in_specs=[pl.BlockSpec((tm, D_MODEL), row), pl.BlockSpec((tm, 1), row)]
        + [_const_spec(c.shape) for c in consts],
        out_specs=[pl.BlockSpec((None, N_HEADS, tm, 2 * LANES), head),
                   pl.BlockSpec((None, N_HEADS, tm, QK_NOPE), head),
                   pl.BlockSpec((None, tm, LANES), lambda b, i: (b, i, 0)),
                   pl.BlockSpec((None, N_HEADS, tm, V_HEAD), head)],
        out_shape=[jax.ShapeDtypeStruct((nb, N_HEADS, seq, 2 * LANES), BF16),
                   jax.ShapeDtypeStruct((nb, N_HEADS, seq, QK_NOPE), BF16),
                   jax.ShapeDtypeStruct((nb, seq, LANES), BF16),
                   jax.ShapeDtypeStruct((nb, N_HEADS, seq, V_HEAD), BF16)],
        compiler_params=pltpu.CompilerParams(
            dimension_semantics=("parallel", "parallel"), vmem_limit_bytes=VMEM_LIMIT_BYTES),
        name="mla_proj",
    )(h, pos, *consts)


def _mla_weights(kv_w_a, kv_w_b, q_w_a, q_w_b):
    pad = jnp.zeros((D_MODEL, PROJ_A_COLS - KV_LORA - Q_LORA - QK_ROPE), F32)
    w_a = jnp.concatenate([kv_w_a[:, :KV_LORA], q_w_a, kv_w_a[:, KV_LORA:], pad], axis=1)
    kv_wb3 = kv_w_b.reshape(KV_LORA, N_HEADS, QK_NOPE + V_HEAD)
    kv_wb = jnp.concatenate([kv_wb3[:, :, :QK_NOPE].reshape(KV_LORA, -1),
                             kv_wb3[:, :, QK_NOPE:].reshape(KV_LORA, -1)], axis=1)
    q_wb3 = q_w_b.reshape(Q_LORA, N_HEADS, QK_NOPE + QK_ROPE)
    q_wb = jnp.concatenate([q_wb3[:, :, :QK_NOPE].reshape(Q_LORA, -1),
                            q_wb3[:, :, QK_NOPE:].reshape(Q_LORA, -1)], axis=1)
    return w_a.astype(BF16), kv_wb.astype(BF16), q_wb.astype(BF16)


def _attn_kernel(q_ref, kn_ref, kr_ref, v_ref, o_ref, s_ref, m_ref, acc_ref):
    tq = tk = ATTN_TQ
    half = tq // 2
    n_q = q_ref.shape[0] // tq
    blocks = [(qi, j) for qi in range(n_q) for j in range(qi + 1)]
    nt = (((1,), (1,)), ((), ()))
    ones = jnp.ones((tk, LANES), BF16)

    def pieces(qi, j):
        return ((0, half, half), (half, tq, tk)) if j == qi else ((0, tq, tk),)

    def scores(n):
        qi, j = blocks[n]
        k = jnp.concatenate([kn_ref[j * tk:(j + 1) * tk, :], kr_ref[j * tk:(j + 1) * tk, :]], axis=1)
        for r0, r1, nk in pieces(qi, j):
            s_ref[n % 2, r0:r1, :nk] = lax.dot_general(
                q_ref[qi * tq + r0:qi * tq + r1, :], k[:nk], nt, preferred_element_type=F32)

    def consume(n):
        qi, j = blocks[n]
        buf = qi % 2
        for r0, r1, nk in pieces(qi, j):
            s = s_ref[n % 2, r0:r1, :nk]
            if j == qi:
                sq = s[:, nk - half:]
                keep = (lax.broadcasted_iota(jnp.int32, sq.shape, 1)
                        <= lax.broadcasted_iota(jnp.int32, sq.shape, 0))
                sq = jnp.where(keep, sq, NEG_INF)
                s = sq if nk == half else jnp.concatenate([s[:, :nk - half], sq], axis=1)
            m_new = jnp.max(s, axis=-1, keepdims=True)
            if j == 0:
                m_new = jnp.broadcast_to(m_new, (r1 - r0, LANES))
            else:
                m_old = m_ref[buf, r0:r1, :]
                m_new = jnp.maximum(m_old, m_new)
            p = jnp.exp(s - jnp.concatenate([m_new] * (nk // LANES), axis=1)).astype(BF16)
            v_ext = jnp.concatenate([v_ref[j * tk:j * tk + nk, :], ones[:nk]], axis=1)
            acc = jnp.dot(p, v_ext, preferred_element_type=F32)
            if j > 0:
                alpha = jnp.exp(m_old - m_new)
                acc = jnp.concatenate([alpha, alpha], axis=1) * acc_ref[buf, r0:r1, :] + acc
            if j == qi:
                o_ref[qi * tq + r0:qi * tq + r1, :] = (
                    acc[:, :V_HEAD] / acc[:, V_HEAD:]).astype(o_ref.dtype)
            else:
                acc_ref[buf, r0:r1, :] = acc
                m_ref[buf, r0:r1, :] = m_new

    scores(0)
    for n in range(len(blocks)):
        if n + 1 < len(blocks):
            scores(n + 1)
        consume(n)


def _attention(q, kn, kr, v):
    nb, nh, seq, _ = q.shape
    assert seq % ATTN_TQ == 0
    return pl.pallas_call(
        _attn_kernel,
        grid=(nb, nh),
        in_specs=[pl.BlockSpec((None, None, seq, 2 * LANES), lambda b, h: (b, h, 0, 0)),
                  pl.BlockSpec((None, None, seq, QK_NOPE), lambda b, h: (b, h, 0, 0)),
                  pl.BlockSpec((None, seq, LANES), lambda b, h: (b, 0, 0)),
                  pl.BlockSpec((None, None, seq, V_HEAD), lambda b, h: (b, h, 0, 0))],
        out_specs=pl.BlockSpec((None, seq, V_HEAD), lambda b, h: (b, 0, h)),
        out_shape=jax.ShapeDtypeStruct((nb, seq, nh * V_HEAD), BF16),
        scratch_shapes=[pltpu.VMEM((2, ATTN_TQ, ATTN_TQ), F32),
                        pltpu.VMEM((2, ATTN_TQ, LANES), F32),
                        pltpu.VMEM((2, ATTN_TQ, V_HEAD + LANES), F32)],
        compiler_params=pltpu.CompilerParams(
            dimension_semantics=("parallel", "parallel"), vmem_limit_bytes=VMEM_LIMIT_BYTES),
        name="attention",
    )(q, kn, kr, v)


def _rope_inv_freq():
    inv_freq = ROPE_THETA ** (-jnp.arange(ROPE_HALF, dtype=F32) / ROPE_HALF)
    group = jnp.concatenate([-inv_freq, inv_freq])
    return jnp.tile(group, LANES // QK_ROPE).reshape(1, LANES)


def _row(v):
    return v.reshape(1, -1).astype(F32)


def kernel(x, positions, ln_mix_g, ln_mix_b, ln_ffn_g, ln_ffn_b, w_ff1, w_ff2, ssm_lam_re, ssm_lam_im, ssm_log_dt, ssm_b_re, ssm_b_im, ssm_c_re, ssm_c_im, ssm_d, ssm_w_glu, ssm_w_out, kv_w_a, kv_norm_g, kv_w_b, q_w_a, q_norm_g, q_w_b, attn_w_o):
    nb, seq, _ = x.shape
    m = nb * seq
    pos = positions.astype(F32).reshape(m, 1)
    inv_freq = _rope_inv_freq()
    w1_all = w_ff1.astype(BF16)
    w2_all = w_ff2.astype(BF16)

    h = x.reshape(m, D_MODEL)
    kn = kr = v = None
    for layer in range(DEPTH):
        ln1 = (_row(ln_mix_g[layer]), _row(ln_mix_b[layer]))
        ln2 = (_row(ln_ffn_g[layer]), _row(ln_ffn_b[layer]))
        if layer < N_A_LAYERS:
            i = layer
            s5_params = _s5_params(ssm_lam_re[i], ssm_lam_im[i], ssm_log_dt[i],
                                   ssm_b_re[i], ssm_b_im[i], ssm_c_re[i], ssm_c_im[i])
            y = _s5_scan(h.reshape(nb, seq, D_MODEL), s5_params, _row(ssm_d[i]))
            h = _row_stage(_s5_tail_mlp_kernel, "s5_tail_mlp", (y.reshape(m, D_MODEL), h),
                           (ssm_w_glu[i].astype(BF16), ssm_w_out[i].astype(BF16)) + ln1,
                           w1_all, w2_all, layer, ln2)
        else:
            j = layer - N_A_LAYERS
            w_a, kv_wb, q_wb = _mla_weights(kv_w_a, kv_w_b, q_w_a[j], q_w_b[j])
            q, kn_j, kr_j, v_j = _mla_proj(h, pos, inv_freq, w_a, _row(kv_norm_g), kv_wb,
                                           _row(q_norm_g[j]), q_wb, nb, seq)
            if layer == N_A_LAYERS:
                kn, kr, v = kn_j, kr_j, v_j
            o = _attention(q, kn, kr, v)
            h = _row_stage(_attn_out_mlp_kernel, "attn_out_mlp", (o.reshape(m, N_HEADS * V_HEAD), h),
                           (attn_w_o[j].astype(BF16),) + ln1, w1_all, w2_all, layer, ln2)
    return h.reshape(nb, seq, D_MODEL)
```

```python
import math

import jax
import numpy as np
import jax.numpy as jnp
from jax import lax
from jax.experimental import pallas as pl
from jax.experimental.pallas import tpu as pltpu

F32 = jnp.float32
BF16 = jnp.bfloat16

D_MODEL = 1024
DEPTH = 2
SSM_GROUP = 16
N_GROUPS = D_MODEL // SSM_GROUP
SSM_STATE = 64
N_HEADS = 8
QK_NOPE = 128
QK_ROPE = 64
V_HEAD = 128
Q_LORA = 384
KV_LORA = 256
ROPE_THETA = 10000.0
SM_SCALE = (QK_NOPE + QK_ROPE) ** -0.5
NEG_INF = -1e30
D_FF = 4 * D_MODEL
N_A_LAYERS = DEPTH // 2
DN_ALPHA = (2 * DEPTH) ** 0.25
LN_EPS = 1e-5
RMS_EPS = 1e-6

LANES = 128
SUBLANES = 8
MXU_DIM = 256
VMEM_LIMIT_BYTES = 56 * 1024 * 1024

S5_L = 8
S5_T = 256
S5_PAIRS = N_GROUPS // 2
S5_N_SLABS = D_MODEL // LANES
S5_SUPER = 4 * LANES
S5_PITCH = S5_T + SUBLANES

ROW_TILE = 512
ATTN_TQ = 512


def _layer_norm(x, g, b):
    mu = jnp.mean(x, axis=-1, keepdims=True)
    xc = x - mu
    var = jnp.mean(xc * xc, axis=-1, keepdims=True)
    return xc * lax.rsqrt(var + LN_EPS) * g + b


def _rms_norm(x, g):
    return x * lax.rsqrt(jnp.mean(x * x, axis=-1, keepdims=True) + RMS_EPS) * g


def _gelu_tanh(x):
    c = math.sqrt(2.0 / math.pi)
    return 0.5 * x * (1.0 + jnp.tanh(c * (x + 0.044715 * (x * x * x))))


def _const_spec(shape):
    nd = len(shape)
    return pl.BlockSpec(shape, lambda *_: (0,) * nd, pipeline_mode=pl.Buffered(1))


def _s5_kernel(x_ref, perm_ref, permt_ref, wi_ref, wt_ref, wo_ref, a8r_ref, a8i_ref, d_ref, o_ref,
               xs_ref, hc_ref):
    nb, t_len, _ = x_ref.shape
    n_chunks = t_len // S5_L
    halves = S5_L * LANES // S5_SUPER

    @pl.when(pl.program_id(0) == 0)
    def _():
        hc_ref[...] = jnp.zeros_like(hc_ref)

    for b in range(nb):
        x_b = x_ref[b]
        for l in range(S5_N_SLABS):
            xs_ref[l, pl.ds(b * S5_PITCH, t_len), :] = x_b[:, l * LANES:(l + 1) * LANES]

    def step_rows(c, j):
        return pl.ds(S5_L * c + j, nb, stride=S5_PITCH)

    def gather(l):
        z_half = []
        for a in range(halves):
            lhs = jnp.concatenate(
                [jnp.concatenate([xs_ref[l, step_rows(c, 4 * a + jj), :] for c in range(n_chunks)],
                                 axis=0) for jj in range(4)], axis=1).astype(BF16)
            z_half.append(jnp.dot(lhs, perm_ref[...], preferred_element_type=F32).astype(BF16))
        return z_half

    def mix(l, z_half):
        y_pair = []
        for gh in range(4):
            q = 4 * l + gh
            z_q = jnp.concatenate([zh[:, gh * LANES:(gh + 1) * LANES] for zh in z_half], axis=1)
            w_in = jnp.concatenate([wi_ref[j, q] for j in range(S5_L)], axis=0)
            w_y = jnp.concatenate([wt_ref[j, q] for j in range(S5_L)] + [wo_ref[0, q], wo_ref[1, q]],
                                  axis=0)
            s_q = jnp.dot(z_q, w_in, preferred_element_type=F32)
            a_re, a_im = a8r_ref[q], a8i_ref[q]
            h_re, h_im = hc_ref[2 * q], hc_ref[2 * q + 1]
            ent_re, ent_im = [], []
            for c in range(n_chunks):
                ent_re.append(h_re)
                ent_im.append(h_im)
                rows_c = slice(c * nb, (c + 1) * nb)
                n_re = a_re * h_re - a_im * h_im + s_q[rows_c, :LANES]
                n_im = a_re * h_im + a_im * h_re + s_q[rows_c, LANES:]
                h_re, h_im = n_re, n_im
            hc_ref[2 * q] = h_re
            hc_ref[2 * q + 1] = h_im
            ent = jnp.concatenate([jnp.concatenate(ent_re, axis=0),
                                   jnp.concatenate(ent_im, axis=0)], axis=1).astype(BF16)
            y_pair.append(jnp.dot(jnp.concatenate([z_q, ent], axis=1), w_y,
                                  preferred_element_type=F32))
        return y_pair

    def emit(l, y_pair):
        d_l = d_ref[:, l * LANES:(l + 1) * LANES]
        for a in range(halves):
            y_l = jnp.concatenate([y[:, a * LANES:(a + 1) * LANES] for y in y_pair], axis=1)
            out = jnp.dot(y_l.astype(BF16), permt_ref[...], preferred_element_type=F32)
            for jj in range(4):
                for c in range(n_chunks):
                    rows = step_rows(c, 4 * a + jj)
                    xs_ref[l, rows, :] = (out[c * nb:(c + 1) * nb, jj * LANES:(jj + 1) * LANES]
                                          + d_l * xs_ref[l, rows, :])
        for b in range(nb):
            o_ref[b, :, l * LANES:(l + 1) * LANES] = _gelu_tanh(
                xs_ref[l, pl.ds(b * S5_PITCH, t_len), :]).astype(o_ref.dtype)

    z_next = gather(0)
    for l in range(S5_N_SLABS):
        y_pair = mix(l, z_next)
        if l + 1 < S5_N_SLABS:
            z_next = gather(l + 1)
        emit(l, y_pair)


def _s5_scan(x, params, d_skip, t_len=S5_T):
    nb, seq, _ = x.shape
    assert nb == SUBLANES and seq % t_len == 0 and t_len % S5_L == 0
    return pl.pallas_call(
        _s5_kernel,
        grid=(seq // t_len,),
        in_specs=[pl.BlockSpec((nb, t_len, D_MODEL), lambda i: (0, i, 0))]
        + [_const_spec(p.shape) for p in params] + [_const_spec(d_skip.shape)],
        out_specs=pl.BlockSpec((nb, t_len, D_MODEL), lambda i: (0, i, 0)),
        out_shape=jax.ShapeDtypeStruct(x.shape, BF16),
        scratch_shapes=[
            pltpu.VMEM((S5_N_SLABS, nb * S5_PITCH, LANES), F32),
            pltpu.VMEM((N_GROUPS, SUBLANES, LANES), F32),
        ],
        compiler_params=pltpu.CompilerParams(
            dimension_semantics=("arbitrary",), vmem_limit_bytes=VMEM_LIMIT_BYTES),
        name="s5_scan",
    )(x, *params, d_skip)


def _s5_params(lam_re, lam_im, log_dt, b_re, b_im, c_re, c_im):
    dt = jnp.exp(log_dt)[:, None]
    mag = jnp.exp(lam_re * dt)
    a_re = mag * jnp.cos(lam_im * dt)
    a_im = mag * jnp.sin(lam_im * dt)
    inv_den = 1.0 / (lam_re * lam_re + lam_im * lam_im)
    coef_re = ((a_re - 1.0) * lam_re + a_im * lam_im) * inv_den
    coef_im = (a_im * lam_re - (a_re - 1.0) * lam_im) * inv_den
    bb_re = coef_re[..., None] * b_re - coef_im[..., None] * b_im
    bb_im = coef_re[..., None] * b_im + coef_im[..., None] * b_re
    pw_re, pw_im = [jnp.ones_like(a_re)], [jnp.zeros_like(a_im)]
    for _ in range(S5_L):
        pw_re, pw_im = (pw_re + [pw_re[-1] * a_re - pw_im[-1] * a_im],
                        pw_im + [pw_re[-1] * a_im + pw_im[-1] * a_re])
    pw_re, pw_im = jnp.stack(pw_re), jnp.stack(pw_im)
    ca_re = c_re[None] * pw_re[:, :, None, :] - c_im[None] * pw_im[:, :, None, :]
    ca_im = c_re[None] * pw_im[:, :, None, :] + c_im[None] * pw_re[:, :, None, :]
    kern = (jnp.einsum('kgop,gpi->kgio', ca_re[:S5_L], bb_re)
            - jnp.einsum('kgop,gpi->kgio', ca_im[:S5_L], bb_im))
    k_c = jnp.transpose(kern, (1, 2, 0, 3)).astype(BF16).reshape(
        S5_PAIRS, 2, SSM_GROUP, LANES)
    rev = S5_L - 1 - np.arange(S5_L)
    rev_re, rev_im = pw_re[rev][..., None], pw_im[rev][..., None]
    ab = jnp.stack([rev_re * bb_re[None] - rev_im * bb_im[None],
                    rev_re * bb_im[None] + rev_im * bb_re[None]])
    a_c = jnp.transpose(ab, (1, 2, 4, 0, 3)).astype(BF16).reshape(
        S5_L, S5_PAIRS, 2, SSM_GROUP, LANES)
    co = jnp.stack([ca_re[1:], -ca_im[1:]])
    c_c = jnp.transpose(co, (0, 2, 4, 1, 3)).astype(BF16).reshape(
        2, S5_PAIRS, 2, SSM_STATE, LANES)
    wt, wi, wo = _s5_weights(k_c, a_c, c_c)

    src = np.arange(S5_SUPER).reshape(4, 4, 2 * SSM_GROUP)
    dst = np.transpose(src, (1, 0, 2)).reshape(-1)
    perm = jnp.asarray(dst[None, :] == np.arange(S5_SUPER)[:, None], BF16)

    def bcast(a):
        return jnp.broadcast_to(a.reshape(S5_PAIRS, 1, LANES), (S5_PAIRS, SUBLANES, LANES))

    return perm, perm.T, wi, wt, wo, bcast(pw_re[S5_L]), bcast(pw_im[S5_L])


def _s5_placements():
    c, p, n = SSM_GROUP, SSM_STATE, S5_L
    shift = np.zeros((n, 2, n, c, n, 2, c), np.float32)
    place_y = np.zeros((2, n, c, n, 2, c), np.float32)
    place_s = np.zeros((2, 2, p, 2, 2, p), np.float32)
    for g in range(2):
        for k in range(n):
            place_y[g, k, :, k, g, :] = np.eye(c)
            for j in range(k + 1):
                shift[j, g, k - j, :, k, g, :] = np.eye(c)
        for part in range(2):
            place_s[g, part, :, part, g, :] = np.eye(p)
    as_bf16 = lambda m, lead: jnp.asarray(m.reshape(lead + (LANES, 2 * LANES)), BF16)
    return as_bf16(shift, (n, 2)), as_bf16(place_s, (2,)), as_bf16(place_y, (2,))


def _s5_weights_kernel(k_ref, a_ref, c_ref, shift_ref, place_s_ref, place_y_ref, wt_ref, wi_ref, wo_ref):
    n_pairs = k_ref.shape[0]
    for g in range(2):
        k_g = k_ref[:, g].reshape(n_pairs * SSM_GROUP, LANES)
        for j in range(S5_L):
            wt_ref[j, :, g] = jnp.dot(k_g, shift_ref[j, g], preferred_element_type=F32).astype(
                wt_ref.dtype).reshape(n_pairs, SSM_GROUP, 2 * LANES)
            a_jg = a_ref[j, :, g].reshape(n_pairs * SSM_GROUP, LANES)
            wi_ref[j, :, g] = jnp.dot(a_jg, place_s_ref[g], preferred_element_type=F32).astype(
                wi_ref.dtype).reshape(n_pairs, SSM_GROUP, 2 * LANES)
        for part in range(2):
            c_pg = c_ref[part, :, g].reshape(n_pairs * SSM_STATE, LANES)
            wo_ref[part, :, g] = jnp.dot(c_pg, place_y_ref[g], preferred_element_type=F32).astype(
                wo_ref.dtype).reshape(n_pairs, SSM_STATE, 2 * LANES)


def _s5_weights(k_c, a_c, c_c):
    wide = 2 * LANES
    wt, wi, wo = pl.pallas_call(
        _s5_weights_kernel,
        out_shape=[jax.ShapeDtypeStruct((S5_L, S5_PAIRS, 2, SSM_GROUP, wide), BF16),
                   jax.ShapeDtypeStruct((S5_L, S5_PAIRS, 2, SSM_GROUP, wide), BF16),
                   jax.ShapeDtypeStruct((2, S5_PAIRS, 2, SSM_STATE, wide), BF16)],
        compiler_params=pltpu.CompilerParams(vmem_limit_bytes=VMEM_LIMIT_BYTES),
        name="s5_weights",
    )(k_c, a_c, c_c, *_s5_placements())
    return (wt.reshape(S5_L, S5_PAIRS, 2 * SSM_GROUP, wide), wi.reshape(S5_L, S5_PAIRS, 2 * SSM_GROUP, wide),
            wo.reshape(2, S5_PAIRS, 2 * SSM_STATE, wide))


ROW_SUBTILES = 2
MLP_CHUNKS = D_FF // D_MODEL


def _skewed(n_stages, stage):
    for step in range(n_stages + ROW_SUBTILES - 1):
        for r in range(ROW_SUBTILES):
            k = step - r
            if 0 <= k < n_stages:
                stage(r, k)


def _sub_rows(ref, r):
    sub = ref.shape[0] // ROW_SUBTILES
    return slice(r * sub, (r + 1) * sub)


def _mlp_stages(d, k, rows, g1_ref, b1_ref, w1_ref, w2_ref, g2_ref, b2_ref, o_ref):
    if k == 0:
        h = _layer_norm(d.pop("pre"), g1_ref[...], b1_ref[...])
        d["hb"] = h.astype(BF16)
        d["acc"] = DN_ALPHA * h
    elif k <= MLP_CHUNKS:
        cols = slice((k - 1) * D_MODEL, k * D_MODEL)
        u = jnp.maximum(jnp.dot(d["hb"], w1_ref[:, cols], preferred_element_type=F32), 0.0)
        d["acc"] = d["acc"] + jnp.dot((u * u).astype(BF16), w2_ref[cols, :],
                                      preferred_element_type=F32)
    else:
        o_ref[rows, :] = _layer_norm(d.pop("acc"), g2_ref[...], b2_ref[...])


def _s5_tail_mlp_kernel(y_ref, x_ref, wg_ref, wo_ref, g1_ref, b1_ref, w1_ref, w2_ref, g2_ref, b2_ref,
                        o_ref):
    state = [dict() for _ in range(ROW_SUBTILES)]

    def stage(r, k):
        d, rows = state[r], _sub_rows(o_ref, r)
        if k == 0:
            d["z"] = jnp.dot(y_ref[rows, :], wg_ref[...], preferred_element_type=F32)
        elif k == 1:
            z = d.pop("z")
            m = (z[:, :D_MODEL] * jax.nn.sigmoid(z[:, D_MODEL:])).astype(BF16)
            d["pre"] = DN_ALPHA * x_ref[rows, :] + jnp.dot(m, wo_ref[...],
                                                           preferred_element_type=F32)
        else:
            _mlp_stages(d, k - 2, rows, g1_ref, b1_ref, w1_ref, w2_ref, g2_ref, b2_ref, o_ref)

    _skewed(MLP_CHUNKS + 4, stage)


def _attn_out_mlp_kernel(a_ref, r_ref, wo_ref, g1_ref, b1_ref, w1_ref, w2_ref, g2_ref, b2_ref, o_ref):
    state = [dict() for _ in range(ROW_SUBTILES)]

    def stage(r, k):
        d, rows = state[r], _sub_rows(o_ref, r)
        if k == 0:
            d["pre"] = DN_ALPHA * r_ref[rows, :] + jnp.dot(a_ref[rows, :], wo_ref[...],
                                                           preferred_element_type=F32)
        else:
            _mlp_stages(d, k - 1, rows, g1_ref, b1_ref, w1_ref, w2_ref, g2_ref, b2_ref, o_ref)

    _skewed(MLP_CHUNKS + 3, stage)


def _row_stage(body, name, row_inputs, const_inputs, w1_all, w2_all, layer, ln2, tm=ROW_TILE):
    m = row_inputs[0].shape[0]
    row = lambda i: (i, 0)
    layer_spec = lambda w: pl.BlockSpec((None,) + w.shape[1:], lambda i: (layer, 0, 0),
                                        pipeline_mode=pl.Buffered(1))
    return pl.pallas_call(
        body,
        grid=(m // tm,),
        in_specs=([pl.BlockSpec((tm, a.shape[1]), row) for a in row_inputs]
                  + [_const_spec(c.shape) for c in const_inputs]
                  + [layer_spec(w1_all), layer_spec(w2_all)]
                  + [_const_spec(c.shape) for c in ln2]),
        out_specs=pl.BlockSpec((tm, D_MODEL), row),
        out_shape=jax.ShapeDtypeStruct((m, D_MODEL), F32),
        compiler_params=pltpu.CompilerParams(
            dimension_semantics=("parallel",), vmem_limit_bytes=VMEM_LIMIT_BYTES),
        name=name,
    )(*row_inputs, *const_inputs, w1_all, w2_all, *ln2)


ROPE_HALF = QK_ROPE // 2
PROJ_A_COLS = 3 * MXU_DIM
Q_ROPE_TILES = N_HEADS * QK_ROPE // LANES


def _mla_proj_kernel(h_ref, pos_ref, invf_ref, wa_ref, kvg_ref, kvwb_ref, qg_ref, qwb_ref,
                     q_ref, kn_ref, kr_ref, v_ref):
    state = [dict() for _ in range(ROW_SUBTILES)]
    c_q0 = KV_LORA
    k_rope0 = KV_LORA + Q_LORA

    def rope(x, d):
        swapped = jnp.where(d["first"], pltpu.roll(x, LANES - ROPE_HALF, axis=1),
                            pltpu.roll(x, ROPE_HALF, axis=1))
        return x * d["cos"] + swapped * d["sin"]

    def stage(r, k):
        d, rows = state[r], _sub_rows(h_ref, r)
        if k == 0:
            d["a"] = jnp.dot(h_ref[rows, :].astype(BF16), wa_ref[...], preferred_element_type=F32)
            ang = pos_ref[rows, :] * invf_ref[...]
            lane = lax.broadcasted_iota(jnp.int32, ang.shape, 1)
            d["first"] = (lane & ROPE_HALF) == 0
            d["low"] = lane < QK_ROPE
            d["cos"] = jnp.cos(ang)
            d["sin"] = jnp.sin(ang)
        elif k == 1:
            a = d.pop("a")
            kr_ref[rows, :] = rope(a[:, k_rope0:k_rope0 + LANES], d).astype(kr_ref.dtype)
            c_kv = _rms_norm(a[:, :KV_LORA], kvg_ref[...]).astype(BF16)
            d["c_q"] = _rms_norm(a[:, c_q0:c_q0 + Q_LORA], qg_ref[...]).astype(BF16)
            d["kv"] = jnp.dot(c_kv, kvwb_ref[...], preferred_element_type=F32)
        elif k == 2:
            kv = d.pop("kv")
            for hd in range(N_HEADS):
                kn_ref[hd, rows, :] = kv[:, hd * QK_NOPE:(hd + 1) * QK_NOPE].astype(kn_ref.dtype)
                v0 = N_HEADS * QK_NOPE + hd * V_HEAD
                v_ref[hd, rows, :] = kv[:, v0:v0 + V_HEAD].astype(v_ref.dtype)
            d["q"] = jnp.dot(d.pop("c_q"), qwb_ref[...], preferred_element_type=F32) * SM_SCALE
        else:
            q = d.pop("q")
            for hd in range(N_HEADS):
                q_ref[hd, rows, :LANES] = q[:, hd * LANES:(hd + 1) * LANES].astype(q_ref.dtype)
            for p in range(Q_ROPE_TILES):
                c0 = N_HEADS * QK_NOPE + p * LANES
                y = rope(q[:, c0:c0 + LANES], d)
                for hd, y_hd in ((2 * p, y), (2 * p + 1, pltpu.roll(y, QK_ROPE, axis=1))):
                    q_ref[hd, rows, LANES:] = jnp.where(d["low"], y_hd, 0.0).astype(q_ref.dtype)

    _skewed(4, stage)


def _mla_proj(h, pos, inv_freq, w_a, kv_g, kv_wb, q_g, q_wb, nb, seq, tm=ROW_TILE):
    nt = seq // tm
    row = lambda b, i: (b * nt + i, 0)
    head = lambda b, i: (b, 0, i, 0)
    consts = (inv_freq, w_a, kv_g, kv_wb, q_g, q_wb)
    return pl.pallas_call(
        _mla_proj_kernel,
        grid=(nb, nt),
        in_specs=[pl.BlockSpec((tm, D_MODEL), row), pl.BlockSpec((tm, 1), row)]
        + [_const_spec(c.shape) for c in consts],
        out_specs=[pl.BlockSpec((None, N_HEADS, tm, 2 * LANES), head),
                   pl.BlockSpec((None, N_HEADS, tm, QK_NOPE), head),
                   pl.BlockSpec((None, tm, LANES), lambda b, i: (b, i, 0)),
                   pl.BlockSpec((None, N_HEADS, tm, V_HEAD), head)],
        out_shape=[jax.ShapeDtypeStruct((nb, N_HEADS, seq, 2 * LANES), BF16),
                   jax.ShapeDtypeStruct((nb, N_HEADS, seq, QK_NOPE), BF16),
                   jax.ShapeDtypeStruct((nb, seq, LANES), BF16),
                   jax.ShapeDtypeStruct((nb, N_HEADS, seq, V_HEAD), BF16)],
        compiler_params=pltpu.CompilerParams(
            dimension_semantics=("parallel", "parallel"), vmem_limit_bytes=VMEM_LIMIT_BYTES),
        name="mla_proj",
    )(h, pos, *consts)


def _mla_weights(kv_w_a, kv_w_b, q_w_a, q_w_b):
    pad = jnp.zeros((D_MODEL, PROJ_A_COLS - KV_LORA - Q_LORA - QK_ROPE), F32)
    w_a = jnp.concatenate([kv_w_a[:, :KV_LORA], q_w_a, kv_w_a[:, KV_LORA:], pad], axis=1)
    kv_wb3 = kv_w_b.reshape(KV_LORA, N_HEADS, QK_NOPE + V_HEAD)
    kv_wb = jnp.concatenate([kv_wb3[:, :, :QK_NOPE].reshape(KV_LORA, -1),
                             kv_wb3[:, :, QK_NOPE:].reshape(KV_LORA, -1)], axis=1)
    q_wb3 = q_w_b.reshape(Q_LORA, N_HEADS, QK_NOPE + QK_ROPE)
    q_wb = jnp.concatenate([q_wb3[:, :, :QK_NOPE].reshape(Q_LORA, -1),
                            q_wb3[:, :, QK_NOPE:].reshape(Q_LORA, -1)], axis=1)
    return w_a.astype(BF16), kv_wb.astype(BF16), q_wb.astype(BF16)


def _attn_kernel(q_ref, kn_ref, kr_ref, v_ref, o_ref, s_ref, m_ref, acc_ref):
    tq = tk = ATTN_TQ
    half = tq // 2
    n_q = q_ref.shape[0] // tq
    blocks = [(qi, j) for qi in range(n_q) for j in range(qi + 1)]
    nt = (((1,), (1,)), ((), ()))
    ones = jnp.ones((tk, LANES), BF16)

    def pieces(qi, j):
        return ((0, half, half), (half, tq, tk)) if j == qi else ((0, tq, tk),)

    def scores(n):
        qi, j = blocks[n]
        k = jnp.concatenate([kn_ref[j * tk:(j + 1) * tk, :], kr_ref[j * tk:(j + 1) * tk, :]], axis=1)
        for r0, r1, nk in pieces(qi, j):
            s_ref[n % 2, r0:r1, :nk] = lax.dot_general(
                q_ref[qi * tq + r0:qi * tq + r1, :], k[:nk], nt, preferred_element_type=F32)

    def consume(n):
        qi, j = blocks[n]
        buf = qi % 2
        for r0, r1, nk in pieces(qi, j):
            s = s_ref[n % 2, r0:r1, :nk]
            if j == qi:
                sq = s[:, nk - half:]
                keep = (lax.broadcasted_iota(jnp.int32, sq.shape, 1)
                        <= lax.broadcasted_iota(jnp.int32, sq.shape, 0))
                sq = jnp.where(keep, sq, NEG_INF)
                s = sq if nk == half else jnp.concatenate([s[:, :nk - half], sq], axis=1)
            m_new = jnp.max(s, axis=-1, keepdims=True)
            if j == 0:
                m_new = jnp.broadcast_to(m_new, (r1 - r0, LANES))
            else:
                m_old = m_ref[buf, r0:r1, :]
                m_new = jnp.maximum(m_old, m_new)
            p = jnp.exp((s - jnp.concatenate([m_new] * (nk // LANES), axis=1)).astype(BF16))
            v_ext = jnp.concatenate([v_ref[j * tk:j * tk + nk, :], ones[:nk]], axis=1)
            acc = jnp.dot(p, v_ext, preferred_element_type=F32)
            if j > 0:
                alpha = jnp.exp(m_old - m_new)
                acc = jnp.concatenate([alpha, alpha], axis=1) * acc_ref[buf, r0:r1, :] + acc
            if j == qi:
                o_ref[qi * tq + r0:qi * tq + r1, :] = (
                    acc[:, :V_HEAD] / acc[:, V_HEAD:]).astype(o_ref.dtype)
            else:
                acc_ref[buf, r0:r1, :] = acc
                m_ref[buf, r0:r1, :] = m_new

    scores(0)
    for n in range(len(blocks)):
        if n + 1 < len(blocks):
            scores(n + 1)
        consume(n)


def _attention(q, kn, kr, v):
    nb, nh, seq, _ = q.shape
    assert seq % ATTN_TQ == 0
    return pl.pallas_call(
        _attn_kernel,
        grid=(nb, nh),
        in_specs=[pl.BlockSpec((None, None, seq, 2 * LANES), lambda b, h: (b, h, 0, 0)),
                  pl.BlockSpec((None, None, seq, QK_NOPE), lambda b, h: (b, h, 0, 0)),
                  pl.BlockSpec((None, seq, LANES), lambda b, h: (b, 0, 0)),
                  pl.BlockSpec((None, None, seq, V_HEAD), lambda b, h: (b, h, 0, 0))],
        out_specs=pl.BlockSpec((None, seq, V_HEAD), lambda b, h: (b, 0, h)),
        out_shape=jax.ShapeDtypeStruct((nb, seq, nh * V_HEAD), BF16),
        scratch_shapes=[pltpu.VMEM((2, ATTN_TQ, ATTN_TQ), F32),
                        pltpu.VMEM((2, ATTN_TQ, LANES), F32),
                        pltpu.VMEM((2, ATTN_TQ, V_HEAD + LANES), F32)],
        compiler_params=pltpu.CompilerParams(
            dimension_semantics=("parallel", "parallel"), vmem_limit_bytes=VMEM_LIMIT_BYTES),
        name="attention",
    )(q, kn, kr, v)


def _rope_inv_freq():
    inv_freq = ROPE_THETA ** (-jnp.arange(ROPE_HALF, dtype=F32) / ROPE_HALF)
    group = jnp.concatenate([-inv_freq, inv_freq])
    return jnp.tile(group, LANES // QK_ROPE).reshape(1, LANES)


def _row(v):
    return v.reshape(1, -1).astype(F32)


def kernel(x, positions, ln_mix_g, ln_mix_b, ln_ffn_g, ln_ffn_b, w_ff1, w_ff2, ssm_lam_re, ssm_lam_im, ssm_log_dt, ssm_b_re, ssm_b_im, ssm_c_re, ssm_c_im, ssm_d, ssm_w_glu, ssm_w_out, kv_w_a, kv_norm_g, kv_w_b, q_w_a, q_norm_g, q_w_b, attn_w_o):
    nb, seq, _ = x.shape
    m = nb * seq
    pos = positions.astype(F32).reshape(m, 1)
    inv_freq = _rope_inv_freq()
    w1_all = w_ff1.astype(BF16)
    w2_all = w_ff2.astype(BF16)

    h = x.reshape(m, D_MODEL)
    kn = kr = v = None
    for layer in range(DEPTH):
        ln1 = (_row(ln_mix_g[layer]), _row(ln_mix_b[layer]))
        ln2 = (_row(ln_ffn_g[layer]), _row(ln_ffn_b[layer]))
        if layer < N_A_LAYERS:
            i = layer
            s5_params = _s5_params(ssm_lam_re[i], ssm_lam_im[i], ssm_log_dt[i],
                                   ssm_b_re[i], ssm_b_im[i], ssm_c_re[i], ssm_c_im[i])
            y = _s5_scan(h.reshape(nb, seq, D_MODEL), s5_params, _row(ssm_d[i]))
            h = _row_stage(_s5_tail_mlp_kernel, "s5_tail_mlp", (y.reshape(m, D_MODEL), h),
                           (ssm_w_glu[i].astype(BF16), ssm_w_out[i].astype(BF16)) + ln1,
                           w1_all, w2_all, layer, ln2)
        else:
            j = layer - N_A_LAYERS
            w_a, kv_wb, q_wb = _mla_weights(kv_w_a, kv_w_b, q_w_a[j], q_w_b[j])
            q, kn_j, kr_j, v_j = _mla_proj(h, pos, inv_freq, w_a, _row(kv_norm_g), kv_wb,
                                           _row(q_norm_g[j]), q_wb, nb, seq)
            if layer == N_A_LAYERS:
                kn, kr, v = kn_j, kr_j, v_j
            o = _attention(q, kn, kr, v)
            h = _row_stage(_attn_out_mlp_kernel, "attn_out_mlp", (o.reshape(m, N_HEADS * V_HEAD), h),
                           (attn_w_o[j].astype(BF16),) + ln1, w1_all, w2_all, layer, ln2)
    return h.reshape(nb, seq, D_MODEL)
```

```python
import math

import jax
import numpy as np
import jax.numpy as jnp
from jax import lax
from jax.experimental import pallas as pl
from jax.experimental.pallas import tpu as pltpu

F32 = jnp.float32
BF16 = jnp.bfloat16

D_MODEL = 1024
DEPTH = 2
SSM_GROUP = 16
N_GROUPS = D_MODEL // SSM_GROUP
SSM_STATE = 64
N_HEADS = 8
QK_NOPE = 128
QK_ROPE = 64
V_HEAD = 128
Q_LORA = 384
KV_LORA = 256
ROPE_THETA = 10000.0
SM_SCALE = (QK_NOPE + QK_ROPE) ** -0.5
NEG_INF = -1e30
D_FF = 4 * D_MODEL
N_A_LAYERS = DEPTH // 2
DN_ALPHA = (2 * DEPTH) ** 0.25
LN_EPS = 1e-5
RMS_EPS = 1e-6

LANES = 128
SUBLANES = 8
MXU_DIM = 256
VMEM_LIMIT_BYTES = 56 * 1024 * 1024
VMEM_LIMIT_FUSED_BYTES = 58 * 1024 * 1024

S5_L = 8
S5_T = 256
S5_PAIRS = N_GROUPS // 2
S5_N_SLABS = D_MODEL // LANES
S5_SUPER = 4 * LANES
S5_PITCH = S5_T + SUBLANES

ROW_TILE = 512
ATTN_TQ = 512


def _layer_norm(x, g, b):
    mu = jnp.mean(x, axis=-1, keepdims=True)
    xc = x - mu
    var = jnp.mean(xc * xc, axis=-1, keepdims=True)
    return xc * lax.rsqrt(var + LN_EPS) * g + b


def _rms_norm(x, g):
    return x * lax.rsqrt(jnp.mean(x * x, axis=-1, keepdims=True) + RMS_EPS) * g


def _gelu_tanh(x):
    c = math.sqrt(2.0 / math.pi)
    return 0.5 * x * (1.0 + jnp.tanh(c * (x + 0.044715 * (x * x * x))))


def _const_spec(shape):
    nd = len(shape)
    return pl.BlockSpec(shape, lambda *_: (0,) * nd, pipeline_mode=pl.Buffered(1))


def _s5_kernel(x_ref, perm_ref, permt_ref, wi_ref, wt_ref, wo_ref, a8r_ref, a8i_ref, d_ref, o_ref,
               xs_ref, hc_ref):
    nb, t_len, _ = x_ref.shape
    n_chunks = t_len // S5_L
    halves = S5_L * LANES // S5_SUPER

    @pl.when(pl.program_id(0) == 0)
    def _():
        hc_ref[...] = jnp.zeros_like(hc_ref)

    for b in range(nb):
        x_b = x_ref[b]
        for l in range(S5_N_SLABS):
            xs_ref[l, pl.ds(b * S5_PITCH, t_len), :] = x_b[:, l * LANES:(l + 1) * LANES]

    def step_rows(c, j):
        return pl.ds(S5_L * c + j, nb, stride=S5_PITCH)

    def gather(l):
        z_half = []
        for a in range(halves):
            lhs = jnp.concatenate(
                [jnp.concatenate([xs_ref[l, step_rows(c, 4 * a + jj), :] for c in range(n_chunks)],
                                 axis=0) for jj in range(4)], axis=1).astype(BF16)
            z_half.append(jnp.dot(lhs, perm_ref[...], preferred_element_type=F32).astype(BF16))
        return z_half

    def mix(l, z_half):
        y_pair = []
        for gh in range(4):
            q = 4 * l + gh
            z_q = jnp.concatenate([zh[:, gh * LANES:(gh + 1) * LANES] for zh in z_half], axis=1)
            w_in = jnp.concatenate([wi_ref[j, q] for j in range(S5_L)], axis=0)
            w_y = jnp.concatenate([wt_ref[j, q] for j in range(S5_L)] + [wo_ref[0, q], wo_ref[1, q]],
                                  axis=0)
            s_q = jnp.dot(z_q, w_in, preferred_element_type=F32)
            a_re, a_im = a8r_ref[q], a8i_ref[q]
            h_re, h_im = hc_ref[2 * q], hc_ref[2 * q + 1]
            ent_re, ent_im = [], []
            for c in range(n_chunks):
                ent_re.append(h_re)
                ent_im.append(h_im)
                rows_c = slice(c * nb, (c + 1) * nb)
                n_re = a_re * h_re - a_im * h_im + s_q[rows_c, :LANES]
                n_im = a_re * h_im + a_im * h_re + s_q[rows_c, LANES:]
                h_re, h_im = n_re, n_im
            hc_ref[2 * q] = h_re
            hc_ref[2 * q + 1] = h_im
            ent = jnp.concatenate([jnp.concatenate(ent_re, axis=0),
                                   jnp.concatenate(ent_im, axis=0)], axis=1).astype(BF16)
            y_pair.append(jnp.dot(jnp.concatenate([z_q, ent], axis=1), w_y,
                                  preferred_element_type=F32))
        return y_pair

    def emit(l, y_pair):
        d_l = d_ref[:, l * LANES:(l + 1) * LANES]
        for a in range(halves):
            y_l = jnp.concatenate([y[:, a * LANES:(a + 1) * LANES] for y in y_pair], axis=1)
            out = jnp.dot(y_l.astype(BF16), permt_ref[...], preferred_element_type=F32)
            for jj in range(4):
                for c in range(n_chunks):
                    rows = step_rows(c, 4 * a + jj)
                    xs_ref[l, rows, :] = (out[c * nb:(c + 1) * nb, jj * LANES:(jj + 1) * LANES]
                                          + d_l * xs_ref[l, rows, :])
        for b in range(nb):
            o_ref[b, :, l * LANES:(l + 1) * LANES] = _gelu_tanh(
                xs_ref[l, pl.ds(b * S5_PITCH, t_len), :]).astype(o_ref.dtype)

    z_next = gather(0)
    for l in range(S5_N_SLABS):
        y_pair = mix(l, z_next)
        if l + 1 < S5_N_SLABS:
            z_next = gather(l + 1)
        emit(l, y_pair)


def _s5_scan(x, params, d_skip, t_len=S5_T):
    nb, seq, _ = x.shape
    assert nb == SUBLANES and seq % t_len == 0 and t_len % S5_L == 0
    return pl.pallas_call(
        _s5_kernel,
        grid=(seq // t_len,),
        in_specs=[pl.BlockSpec((nb, t_len, D_MODEL), lambda i: (0, i, 0))]
        + [_const_spec(p.shape) for p in params] + [_const_spec(d_skip.shape)],
        out_specs=pl.BlockSpec((nb, t_len, D_MODEL), lambda i: (0, i, 0)),
        out_shape=jax.ShapeDtypeStruct(x.shape, BF16),
        scratch_shapes=[
            pltpu.VMEM((S5_N_SLABS, nb * S5_PITCH, LANES), F32),
            pltpu.VMEM((N_GROUPS, SUBLANES, LANES), F32),
        ],
        compiler_params=pltpu.CompilerParams(
            dimension_semantics=("arbitrary",), vmem_limit_bytes=VMEM_LIMIT_BYTES),
        name="s5_scan",
    )(x, *params, d_skip)


def _s5_params(lam_re, lam_im, log_dt, b_re, b_im, c_re, c_im):
    dt = jnp.exp(log_dt)[:, None]
    mag = jnp.exp(lam_re * dt)
    a_re = mag * jnp.cos(lam_im * dt)
    a_im = mag * jnp.sin(lam_im * dt)
    inv_den = 1.0 / (lam_re * lam_re + lam_im * lam_im)
    coef_re = ((a_re - 1.0) * lam_re + a_im * lam_im) * inv_den
    coef_im = (a_im * lam_re - (a_re - 1.0) * lam_im) * inv_den
    bb_re = coef_re[..., None] * b_re - coef_im[..., None] * b_im
    bb_im = coef_re[..., None] * b_im + coef_im[..., None] * b_re
    pw_re, pw_im = [jnp.ones_like(a_re)], [jnp.zeros_like(a_im)]
    for _ in range(S5_L):
        pw_re, pw_im = (pw_re + [pw_re[-1] * a_re - pw_im[-1] * a_im],
                        pw_im + [pw_re[-1] * a_im + pw_im[-1] * a_re])
    pw_re, pw_im = jnp.stack(pw_re), jnp.stack(pw_im)
    ca_re = c_re[None] * pw_re[:, :, None, :] - c_im[None] * pw_im[:, :, None, :]
    ca_im = c_re[None] * pw_im[:, :, None, :] + c_im[None] * pw_re[:, :, None, :]
    kern = (jnp.einsum('kgop,gpi->kgio', ca_re[:S5_L], bb_re)
            - jnp.einsum('kgop,gpi->kgio', ca_im[:S5_L], bb_im))
    k_c = jnp.transpose(kern, (1, 2, 0, 3)).astype(BF16).reshape(
        S5_PAIRS, 2, SSM_GROUP, LANES)
    rev = S5_L - 1 - np.arange(S5_L)
    rev_re, rev_im = pw_re[rev][..., None], pw_im[rev][..., None]
    ab = jnp.stack([rev_re * bb_re[None] - rev_im * bb_im[None],
                    rev_re * bb_im[None] + rev_im * bb_re[None]])
    a_c = jnp.transpose(ab, (1, 2, 4, 0, 3)).astype(BF16).reshape(
        S5_L, S5_PAIRS, 2, SSM_GROUP, LANES)
    co = jnp.stack([ca_re[1:], -ca_im[1:]])
    c_c = jnp.transpose(co, (0, 2, 4, 1, 3)).astype(BF16).reshape(
        2, S5_PAIRS, 2, SSM_STATE, LANES)
    wt, wi, wo = _s5_weights(k_c, a_c, c_c)

    src = np.arange(S5_SUPER).reshape(4, 4, 2 * SSM_GROUP)
    dst = np.transpose(src, (1, 0, 2)).reshape(-1)
    perm = jnp.asarray(dst[None, :] == np.arange(S5_SUPER)[:, None], BF16)

    def bcast(a):
        return jnp.broadcast_to(a.reshape(S5_PAIRS, 1, LANES), (S5_PAIRS, SUBLANES, LANES))

    return perm, perm.T, wi, wt, wo, bcast(pw_re[S5_L]), bcast(pw_im[S5_L])


def _s5_placements():
    c, p, n = SSM_GROUP, SSM_STATE, S5_L
    shift = np.zeros((n, 2, n, c, n, 2, c), np.float32)
    place_y = np.zeros((2, n, c, n, 2, c), np.float32)
    place_s = np.zeros((2, 2, p, 2, 2, p), np.float32)
    for g in range(2):
        for k in range(n):
            place_y[g, k, :, k, g, :] = np.eye(c)
            for j in range(k + 1):
                shift[j, g, k - j, :, k, g, :] = np.eye(c)
        for part in range(2):
            place_s[g, part, :, part, g, :] = np.eye(p)
    as_bf16 = lambda m, lead: jnp.asarray(m.reshape(lead + (LANES, 2 * LANES)), BF16)
    return as_bf16(shift, (n, 2)), as_bf16(place_s, (2,)), as_bf16(place_y, (2,))


def _s5_weights_kernel(k_ref, a_ref, c_ref, shift_ref, place_s_ref, place_y_ref, wt_ref, wi_ref, wo_ref):
    n_pairs = k_ref.shape[0]
    for g in range(2):
        k_g = k_ref[:, g].reshape(n_pairs * SSM_GROUP, LANES)
        for j in range(S5_L):
            wt_ref[j, :, g] = jnp.dot(k_g, shift_ref[j, g], preferred_element_type=F32).astype(
                wt_ref.dtype).reshape(n_pairs, SSM_GROUP, 2 * LANES)
            a_jg = a_ref[j, :, g].reshape(n_pairs * SSM_GROUP, LANES)
            wi_ref[j, :, g] = jnp.dot(a_jg, place_s_ref[g], preferred_element_type=F32).astype(
                wi_ref.dtype).reshape(n_pairs, SSM_GROUP, 2 * LANES)
        for part in range(2):
            c_pg = c_ref[part, :, g].reshape(n_pairs * SSM_STATE, LANES)
            wo_ref[part, :, g] = jnp.dot(c_pg, place_y_ref[g], preferred_element_type=F32).astype(
                wo_ref.dtype).reshape(n_pairs, SSM_STATE, 2 * LANES)


def _s5_weights(k_c, a_c, c_c):
    wide = 2 * LANES
    wt, wi, wo = pl.pallas_call(
        _s5_weights_kernel,
        out_shape=[jax.ShapeDtypeStruct((S5_L, S5_PAIRS, 2, SSM_GROUP, wide), BF16),
                   jax.ShapeDtypeStruct((S5_L, S5_PAIRS, 2, SSM_GROUP, wide), BF16),
                   jax.ShapeDtypeStruct((2, S5_PAIRS, 2, SSM_STATE, wide), BF16)],
        compiler_params=pltpu.CompilerParams(vmem_limit_bytes=VMEM_LIMIT_BYTES),
        name="s5_weights",
    )(k_c, a_c, c_c, *_s5_placements())
    return (wt.reshape(S5_L, S5_PAIRS, 2 * SSM_GROUP, wide), wi.reshape(S5_L, S5_PAIRS, 2 * SSM_GROUP, wide),
            wo.reshape(2, S5_PAIRS, 2 * SSM_STATE, wide))


ROW_SUBTILES = 2
MLP_CHUNKS = D_FF // D_MODEL


def _skewed(n_stages, stage):
    for step in range(n_stages + ROW_SUBTILES - 1):
        for r in range(ROW_SUBTILES):
            k = step - r
            if 0 <= k < n_stages:
                stage(r, k)


def _sub_rows(ref, r):
    sub = ref.shape[0] // ROW_SUBTILES
    return slice(r * sub, (r + 1) * sub)


def _mlp_stages(d, k, rows, g1_ref, b1_ref, w1_ref, w2_ref, g2_ref, b2_ref, o_ref):
    if k == 0:
        h = _layer_norm(d.pop("pre"), g1_ref[...], b1_ref[...])
        d["hb"] = h.astype(BF16)
        d["acc"] = DN_ALPHA * h
    elif k <= MLP_CHUNKS:
        cols = slice((k - 1) * D_MODEL, k * D_MODEL)
        u = jnp.maximum(jnp.dot(d["hb"], w1_ref[:, cols], preferred_element_type=F32), 0.0)
        d["acc"] = d["acc"] + jnp.dot((u * u).astype(BF16), w2_ref[cols, :],
                                      preferred_element_type=F32)
    else:
        d["out"] = _layer_norm(d.pop("acc"), g2_ref[...], b2_ref[...])
        o_ref[rows, :] = d["out"]


def _s5_tail_mlp_proj_kernel(y_ref, x_ref, pos_ref, wg_ref, wo_ref, g1_ref, b1_ref, w1_ref, w2_ref,
                             g2_ref, b2_ref, invf_ref, wa_ref, kvg_ref, kvwb_ref, qg_ref, qwb_ref,
                             o_ref, q_ref, kn_ref, kr_ref, v_ref):
    state = [dict() for _ in range(ROW_SUBTILES)]
    n_mlp = MLP_CHUNKS + 2

    def stage(r, k):
        d, rows = state[r], _sub_rows(o_ref, r)
        if k == 0:
            d["z"] = jnp.dot(y_ref[rows, :], wg_ref[...], preferred_element_type=F32)
        elif k == 1:
            z = d.pop("z")
            m = (z[:, :D_MODEL] * jax.nn.sigmoid(z[:, D_MODEL:])).astype(BF16)
            d["pre"] = DN_ALPHA * x_ref[rows, :] + jnp.dot(m, wo_ref[...],
                                                           preferred_element_type=F32)
        elif k < 2 + n_mlp:
            _mlp_stages(d, k - 2, rows, g1_ref, b1_ref, w1_ref, w2_ref, g2_ref, b2_ref, o_ref)
        else:
            _proj_stages(d, k - 2 - n_mlp, rows, pos_ref, invf_ref, wa_ref, kvg_ref, kvwb_ref,
                         qg_ref, qwb_ref, q_ref, kn_ref, kr_ref, v_ref)

    _skewed(2 + n_mlp + PROJ_STAGES, stage)


def _attn_out_mlp_kernel(a_ref, r_ref, wo_ref, g1_ref, b1_ref, w1_ref, w2_ref, g2_ref, b2_ref, o_ref):
    state = [dict() for _ in range(ROW_SUBTILES)]

    def stage(r, k):
        d, rows = state[r], _sub_rows(o_ref, r)
        if k == 0:
            d["pre"] = DN_ALPHA * r_ref[rows, :] + jnp.dot(a_ref[rows, :], wo_ref[...],
                                                           preferred_element_type=F32)
        else:
            _mlp_stages(d, k - 1, rows, g1_ref, b1_ref, w1_ref, w2_ref, g2_ref, b2_ref, o_ref)

    _skewed(MLP_CHUNKS + 3, stage)


def _row_stage(body, name, row_inputs, const_inputs, w1_all, w2_all, layer, ln2, tm=ROW_TILE):
    m = row_inputs[0].shape[0]
    row = lambda i: (i, 0)
    layer_spec = lambda w: pl.BlockSpec((None,) + w.shape[1:], lambda i: (layer, 0, 0),
                                        pipeline_mode=pl.Buffered(1))
    return pl.pallas_call(
        body,
        grid=(m // tm,),
        in_specs=([pl.BlockSpec((tm, a.shape[1]), row) for a in row_inputs]
                  + [_const_spec(c.shape) for c in const_inputs]
                  + [layer_spec(w1_all), layer_spec(w2_all)]
                  + [_const_spec(c.shape) for c in ln2]),
        out_specs=pl.BlockSpec((tm, D_MODEL), row),
        out_shape=jax.ShapeDtypeStruct((m, D_MODEL), F32),
        compiler_params=pltpu.CompilerParams(
            dimension_semantics=("parallel",), vmem_limit_bytes=VMEM_LIMIT_BYTES),
        name=name,
    )(*row_inputs, *const_inputs, w1_all, w2_all, *ln2)


ROPE_HALF = QK_ROPE // 2
PROJ_A_COLS = 3 * MXU_DIM
Q_ROPE_TILES = N_HEADS * QK_ROPE // LANES


PROJ_STAGES = 4


def _proj_stages(d, k, rows, pos_ref, invf_ref, wa_ref, kvg_ref, kvwb_ref, qg_ref, qwb_ref,
                 q_ref, kn_ref, kr_ref, v_ref):
    c_q0 = KV_LORA
    k_rope0 = KV_LORA + Q_LORA

    def rope(x):
        swapped = jnp.where(d["first"], pltpu.roll(x, LANES - ROPE_HALF, axis=1),
                            pltpu.roll(x, ROPE_HALF, axis=1))
        return x * d["cos"] + swapped * d["sin"]

    if k == 0:
        d["a"] = jnp.dot(d.pop("out").astype(BF16), wa_ref[...], preferred_element_type=F32)
        ang = pos_ref[rows, :] * invf_ref[...]
        lane = lax.broadcasted_iota(jnp.int32, ang.shape, 1)
        d["first"] = (lane & ROPE_HALF) == 0
        d["low"] = lane < QK_ROPE
        d["cos"] = jnp.cos(ang)
        d["sin"] = jnp.sin(ang)
    elif k == 1:
        a = d.pop("a")
        kr_ref[rows, :] = rope(a[:, k_rope0:k_rope0 + LANES]).astype(kr_ref.dtype)
        c_kv = _rms_norm(a[:, :KV_LORA], kvg_ref[...]).astype(BF16)
        d["c_q"] = _rms_norm(a[:, c_q0:c_q0 + Q_LORA], qg_ref[...]).astype(BF16)
        d["kv"] = jnp.dot(c_kv, kvwb_ref[...], preferred_element_type=F32)
    elif k == 2:
        kv = d.pop("kv")
        for hd in range(N_HEADS):
            kn_ref[hd, rows, :] = kv[:, hd * QK_NOPE:(hd + 1) * QK_NOPE].astype(kn_ref.dtype)
            v0 = N_HEADS * QK_NOPE + hd * V_HEAD
            v_ref[hd, rows, :] = kv[:, v0:v0 + V_HEAD].astype(v_ref.dtype)
        d["q"] = jnp.dot(d.pop("c_q"), qwb_ref[...], preferred_element_type=F32) * SM_SCALE
    else:
        q = d.pop("q")
        for hd in range(N_HEADS):
            q_ref[hd, rows, :LANES] = q[:, hd * LANES:(hd + 1) * LANES].astype(q_ref.dtype)
        for p in range(Q_ROPE_TILES):
            c0 = N_HEADS * QK_NOPE + p * LANES
            y = rope(q[:, c0:c0 + LANES])
            for hd, y_hd in ((2 * p, y), (2 * p + 1, pltpu.roll(y, QK_ROPE, axis=1))):
                q_ref[hd, rows, LANES:] = jnp.where(d["low"], y_hd, 0.0).astype(q_ref.dtype)


def _s5_tail_mlp_proj(y, x, pos, tail_consts, w1_all, w2_all, layer, ln2, proj_consts, nb, seq,
                      tm=ROW_TILE):
    nt = seq // tm
    row = lambda b, i: (b * nt + i, 0)
    head = lambda b, i: (b, 0, i, 0)
    layer_spec = lambda w: pl.BlockSpec((None,) + w.shape[1:], lambda b, i: (layer, 0, 0),
                                        pipeline_mode=pl.Buffered(1))
    consts = lambda cs: [_const_spec(c.shape) for c in cs]
    return pl.pallas_call(
        _s5_tail_mlp_proj_kernel,
        grid=(nb, nt),
        in_specs=([pl.BlockSpec((tm, D_MODEL), row), pl.BlockSpec((tm, D_MODEL), row),
                   pl.BlockSpec((tm, 1), row)]
                  + consts(tail_consts) + [layer_spec(w1_all), layer_spec(w2_all)]
                  + consts(ln2) + consts(proj_consts)),
        out_specs=[pl.BlockSpec((tm, D_MODEL), row),
                   pl.BlockSpec((None, N_HEADS, tm, 2 * LANES), head),
                   pl.BlockSpec((None, N_HEADS, tm, QK_NOPE), head),
                   pl.BlockSpec((None, tm, LANES), lambda b, i: (b, i, 0)),
                   pl.BlockSpec((None, N_HEADS, tm, V_HEAD), head)],
        out_shape=[jax.ShapeDtypeStruct((nb * seq, D_MODEL), F32),
                   jax.ShapeDtypeStruct((nb, N_HEADS, seq, 2 * LANES), BF16),
                   jax.ShapeDtypeStruct((nb, N_HEADS, seq, QK_NOPE), BF16),
                   jax.ShapeDtypeStruct((nb, seq, LANES), BF16),
                   jax.ShapeDtypeStruct((nb, N_HEADS, seq, V_HEAD), BF16)],
        compiler_params=pltpu.CompilerParams(
            dimension_semantics=("parallel", "parallel"), vmem_limit_bytes=VMEM_LIMIT_FUSED_BYTES),
        name="s5_tail_mlp_proj",
    )(y, x, pos, *tail_consts, w1_all, w2_all, *ln2, *proj_consts)


def _mla_weights(kv_w_a, kv_w_b, q_w_a, q_w_b):
    pad = jnp.zeros((D_MODEL, PROJ_A_COLS - KV_LORA - Q_LORA - QK_ROPE), F32)
    w_a = jnp.concatenate([kv_w_a[:, :KV_LORA], q_w_a, kv_w_a[:, KV_LORA:], pad], axis=1)
    kv_wb3 = kv_w_b.reshape(KV_LORA, N_HEADS, QK_NOPE + V_HEAD)
    kv_wb = jnp.concatenate([kv_wb3[:, :, :QK_NOPE].reshape(KV_LORA, -1),
                             kv_wb3[:, :, QK_NOPE:].reshape(KV_LORA, -1)], axis=1)
    q_wb3 = q_w_b.reshape(Q_LORA, N_HEADS, QK_NOPE + QK_ROPE)
    q_wb = jnp.concatenate([q_wb3[:, :, :QK_NOPE].reshape(Q_LORA, -1),
                            q_wb3[:, :, QK_NOPE:].reshape(Q_LORA, -1)], axis=1)
    return w_a.astype(BF16), kv_wb.astype(BF16), q_wb.astype(BF16)


def _attn_kernel(q_ref, kn_ref, kr_ref, v_ref, o_ref, s_ref, m_ref, acc_ref):
    tq = tk = ATTN_TQ
    half = tq // 2
    n_q = q_ref.shape[0] // tq
    blocks = [(qi, j) for qi in range(n_q) for j in range(qi + 1)]
    nt = (((1,), (1,)), ((), ()))
    ones = jnp.ones((tk, LANES), BF16)

    def pieces(qi, j):
        return ((0, half, half), (half, tq, tk)) if j == qi else ((0, tq, tk),)

    def scores(n):
        qi, j = blocks[n]
        k = jnp.concatenate([kn_ref[j * tk:(j + 1) * tk, :], kr_ref[j * tk:(j + 1) * tk, :]], axis=1)
        for r0, r1, nk in pieces(qi, j):
            s_ref[n % 2, r0:r1, :nk] = lax.dot_general(
                q_ref[qi * tq + r0:qi * tq + r1, :], k[:nk], nt, preferred_element_type=F32)

    def consume(n):
        qi, j = blocks[n]
        buf = qi % 2
        for r0, r1, nk in pieces(qi, j):
            s = s_ref[n % 2, r0:r1, :nk]
            if j == qi:
                sq = s[:, nk - half:]
                keep = (lax.broadcasted_iota(jnp.int32, sq.shape, 1)
                        <= lax.broadcasted_iota(jnp.int32, sq.shape, 0))
                sq = jnp.where(keep, sq, NEG_INF)
                s = sq if nk == half else jnp.concatenate([s[:, :nk - half], sq], axis=1)
            m_new = jnp.max(s, axis=-1, keepdims=True)
            if j == 0:
                m_new = jnp.broadcast_to(m_new, (r1 - r0, LANES))
            else:
                m_old = m_ref[buf, r0:r1, :]
                m_new = jnp.maximum(m_old, m_new)
            p = jnp.exp((s - jnp.concatenate([m_new] * (nk // LANES), axis=1)).astype(BF16))
            v_ext = jnp.concatenate([v_ref[j * tk:j * tk + nk, :], ones[:nk]], axis=1)
            acc = jnp.dot(p, v_ext, preferred_element_type=F32)
            if j > 0:
                alpha = jnp.exp(m_old - m_new)
                acc = jnp.concatenate([alpha, alpha], axis=1) * acc_ref[buf, r0:r1, :] + acc
            if j == qi:
                o_ref[qi * tq + r0:qi * tq + r1, :] = (
                    acc[:, :V_HEAD] / acc[:, V_HEAD:]).astype(o_ref.dtype)
            else:
                acc_ref[buf, r0:r1, :] = acc
                m_ref[buf, r0:r1, :] = m_new

    scores(0)
    for n in range(len(blocks)):
        if n + 1 < len(blocks):
            scores(n + 1)
        consume(n)


def _attention(q, kn, kr, v):
    nb, nh, seq, _ = q.shape
    assert seq % ATTN_TQ == 0
    return pl.pallas_call(
        _attn_kernel,
        grid=(nb, nh),
        in_specs=[pl.BlockSpec((None, None, seq, 2 * LANES), lambda b, h: (b, h, 0, 0)),
                  pl.BlockSpec((None, None, seq, QK_NOPE), lambda b, h: (b, h, 0, 0)),
                  pl.BlockSpec((None, seq, LANES), lambda b, h: (b, 0, 0)),
                  pl.BlockSpec((None, None, seq, V_HEAD), lambda b, h: (b, h, 0, 0))],
        out_specs=pl.BlockSpec((None, seq, V_HEAD), lambda b, h: (b, 0, h)),
        out_shape=jax.ShapeDtypeStruct((nb, seq, nh * V_HEAD), BF16),
        scratch_shapes=[pltpu.VMEM((2, ATTN_TQ, ATTN_TQ), F32),
                        pltpu.VMEM((2, ATTN_TQ, LANES), F32),
                        pltpu.VMEM((2, ATTN_TQ, V_HEAD + LANES), F32)],
        compiler_params=pltpu.CompilerParams(
            dimension_semantics=("parallel", "parallel"), vmem_limit_bytes=VMEM_LIMIT_BYTES),
        name="attention",
    )(q, kn, kr, v)


def _rope_inv_freq():
    inv_freq = ROPE_THETA ** (-jnp.arange(ROPE_HALF, dtype=F32) / ROPE_HALF)
    group = jnp.concatenate([-inv_freq, inv_freq])
    return jnp.tile(group, LANES // QK_ROPE).reshape(1, LANES)


def _row(v):
    return v.reshape(1, -1).astype(F32)


def kernel(x, positions, ln_mix_g, ln_mix_b, ln_ffn_g, ln_ffn_b, w_ff1, w_ff2, ssm_lam_re, ssm_lam_im, ssm_log_dt, ssm_b_re, ssm_b_im, ssm_c_re, ssm_c_im, ssm_d, ssm_w_glu, ssm_w_out, kv_w_a, kv_norm_g, kv_w_b, q_w_a, q_norm_g, q_w_b, attn_w_o):
    nb, seq, _ = x.shape
    m = nb * seq
    pos = positions.astype(F32).reshape(m, 1)
    inv_freq = _rope_inv_freq()
    w1_all = w_ff1.astype(BF16)
    w2_all = w_ff2.astype(BF16)

    assert N_A_LAYERS == 1 and DEPTH == 2
    h = x.reshape(m, D_MODEL)
    ln1 = [(_row(ln_mix_g[k]), _row(ln_mix_b[k])) for k in range(DEPTH)]
    ln2 = [(_row(ln_ffn_g[k]), _row(ln_ffn_b[k])) for k in range(DEPTH)]

    s5_params = _s5_params(ssm_lam_re[0], ssm_lam_im[0], ssm_log_dt[0],
                           ssm_b_re[0], ssm_b_im[0], ssm_c_re[0], ssm_c_im[0])
    y = _s5_scan(h.reshape(nb, seq, D_MODEL), s5_params, _row(ssm_d[0]))
    w_a, kv_wb, q_wb = _mla_weights(kv_w_a, kv_w_b, q_w_a[0], q_w_b[0])
    h, q, kn, kr, v = _s5_tail_mlp_proj(
        y.reshape(m, D_MODEL), h, pos,
        (ssm_w_glu[0].astype(BF16), ssm_w_out[0].astype(BF16)) + ln1[0], w1_all, w2_all, 0, ln2[0],
        (inv_freq, w_a, _row(kv_norm_g), kv_wb, _row(q_norm_g[0]), q_wb), nb, seq)
    o = _attention(q, kn, kr, v)
    h = _row_stage(_attn_out_mlp_kernel, "attn_out_mlp", (o.reshape(m, N_HEADS * V_HEAD), h),
                   (attn_w_o[0].astype(BF16),) + ln1[1], w1_all, w2_all, 1, ln2[1], tm=2 * ROW_TILE)
    return h.reshape(nb, seq, D_MODEL)
```

```python
import math

import jax
import numpy as np
import jax.numpy as jnp
from jax import lax
from jax.experimental import pallas as pl
from jax.experimental.pallas import tpu as pltpu

F32 = jnp.float32
BF16 = jnp.bfloat16

D_MODEL = 1024
DEPTH = 2
SSM_GROUP = 16
N_GROUPS = D_MODEL // SSM_GROUP
SSM_STATE = 64
N_HEADS = 8
QK_NOPE = 128
QK_ROPE = 64
V_HEAD = 128
Q_LORA = 384
KV_LORA = 256
ROPE_THETA = 10000.0
SM_SCALE = (QK_NOPE + QK_ROPE) ** -0.5
NEG_INF = -1e30
D_FF = 4 * D_MODEL
N_A_LAYERS = DEPTH // 2
DN_ALPHA = (2 * DEPTH) ** 0.25
LN_EPS = 1e-5
RMS_EPS = 1e-6

LANES = 128
SUBLANES = 8
MXU_DIM = 256
VMEM_LIMIT_BYTES = 56 * 1024 * 1024
VMEM_LIMIT_FUSED_BYTES = 58 * 1024 * 1024

S5_L = 8
S5_T = 256
S5_PAIRS = N_GROUPS // 2
S5_N_SLABS = D_MODEL // LANES
S5_SUPER = 4 * LANES
S5_PITCH = S5_T + SUBLANES

ROW_TILE = 512
ATTN_TQ = 512
ATTN_TK = 1024


def _layer_norm(x, g, b):
    mu = jnp.mean(x, axis=-1, keepdims=True)
    xc = x - mu
    var = jnp.mean(xc * xc, axis=-1, keepdims=True)
    return xc * lax.rsqrt(var + LN_EPS) * g + b


def _rms_norm(x, g):
    return x * lax.rsqrt(jnp.mean(x * x, axis=-1, keepdims=True) + RMS_EPS) * g


def _gelu_tanh(x):
    c = math.sqrt(2.0 / math.pi)
    half_x = 0.5 * x
    return half_x + half_x * jnp.tanh(x * (c + (0.044715 * c) * (x * x)))


def _const_spec(shape):
    nd = len(shape)
    return pl.BlockSpec(shape, lambda *_: (0,) * nd, pipeline_mode=pl.Buffered(1))


def _s5_kernel(x_ref, perm_ref, permt_ref, wi_ref, wt_ref, wo_ref, a8r_ref, a8i_ref, d_ref, o_ref,
               xs_ref, hc_ref):
    nb, t_len, _ = x_ref.shape
    n_chunks = t_len // S5_L
    halves = S5_L * LANES // S5_SUPER

    @pl.when(pl.program_id(0) == 0)
    def _():
        hc_ref[...] = jnp.zeros_like(hc_ref)

    for b in range(nb):
        x_b = x_ref[b]
        for l in range(S5_N_SLABS):
            xs_ref[l, pl.ds(b * S5_PITCH, t_len), :] = x_b[:, l * LANES:(l + 1) * LANES]

    def step_rows(c, j):
        return pl.ds(S5_L * c + j, nb, stride=S5_PITCH)

    def gather(l):
        z_half = []
        for a in range(halves):
            lhs = jnp.concatenate(
                [jnp.concatenate([xs_ref[l, step_rows(c, 4 * a + jj), :] for c in range(n_chunks)],
                                 axis=0) for jj in range(4)], axis=1).astype(BF16)
            z_half.append(jnp.dot(lhs, perm_ref[...], preferred_element_type=F32).astype(BF16))
        return z_half

    def mix(l, z_half):
        y_pair = []
        for gh in range(4):
            q = 4 * l + gh
            z_q = jnp.concatenate([zh[:, gh * LANES:(gh + 1) * LANES] for zh in z_half], axis=1)
            w_in = jnp.concatenate([wi_ref[j, q] for j in range(S5_L)], axis=0)
            w_y = jnp.concatenate([wt_ref[j, q] for j in range(S5_L)] + [wo_ref[0, q], wo_ref[1, q]],
                                  axis=0)
            s_q = jnp.dot(z_q, w_in, preferred_element_type=F32)
            a_re, a_im = a8r_ref[q], a8i_ref[q]
            h_re, h_im = hc_ref[2 * q], hc_ref[2 * q + 1]
            ent_re, ent_im = [], []
            for c in range(n_chunks):
                ent_re.append(h_re)
                ent_im.append(h_im)
                rows_c = slice(c * nb, (c + 1) * nb)
                n_re = a_re * h_re - a_im * h_im + s_q[rows_c, :LANES]
                n_im = a_re * h_im + a_im * h_re + s_q[rows_c, LANES:]
                h_re, h_im = n_re, n_im
            hc_ref[2 * q] = h_re
            hc_ref[2 * q + 1] = h_im
            ent = jnp.concatenate([jnp.concatenate(ent_re, axis=0),
                                   jnp.concatenate(ent_im, axis=0)], axis=1).astype(BF16)
            y_pair.append(jnp.dot(jnp.concatenate([z_q, ent], axis=1), w_y,
                                  preferred_element_type=F32))
        return y_pair

    def emit(l, y_pair):
        d_l = d_ref[:, l * LANES:(l + 1) * LANES]
        for a in range(halves):
            y_l = jnp.concatenate([y[:, a * LANES:(a + 1) * LANES] for y in y_pair], axis=1)
            out = jnp.dot(y_l.astype(BF16), permt_ref[...], preferred_element_type=F32)
            for jj in range(4):
                for c in range(n_chunks):
                    rows = step_rows(c, 4 * a + jj)
                    xs_ref[l, rows, :] = (out[c * nb:(c + 1) * nb, jj * LANES:(jj + 1) * LANES]
                                          + d_l * xs_ref[l, rows, :])
        for b in range(nb):
            o_ref[b, :, l * LANES:(l + 1) * LANES] = _gelu_tanh(
                xs_ref[l, pl.ds(b * S5_PITCH, t_len), :]).astype(o_ref.dtype)

    z_next = gather(0)
    for l in range(S5_N_SLABS):
        y_pair = mix(l, z_next)
        if l + 1 < S5_N_SLABS:
            z_next = gather(l + 1)
        emit(l, y_pair)


def _s5_scan(x, params, d_skip, t_len=S5_T):
    nb, seq, _ = x.shape
    assert nb == SUBLANES and seq % t_len == 0 and t_len % S5_L == 0
    return pl.pallas_call(
        _s5_kernel,
        grid=(seq // t_len,),
        in_specs=[pl.BlockSpec((nb, t_len, D_MODEL), lambda i: (0, i, 0))]
        + [_const_spec(p.shape) for p in params] + [_const_spec(d_skip.shape)],
        out_specs=pl.BlockSpec((nb, t_len, D_MODEL), lambda i: (0, i, 0)),
        out_shape=jax.ShapeDtypeStruct(x.shape, BF16),
        scratch_shapes=[
            pltpu.VMEM((S5_N_SLABS, nb * S5_PITCH, LANES), F32),
            pltpu.VMEM((N_GROUPS, SUBLANES, LANES), F32),
        ],
        compiler_params=pltpu.CompilerParams(
            dimension_semantics=("arbitrary",), vmem_limit_bytes=VMEM_LIMIT_BYTES),
        name="s5_scan",
    )(x, *params, d_skip)


def _s5_params(lam_re, lam_im, log_dt, b_re, b_im, c_re, c_im):
    dt = jnp.exp(log_dt)[:, None]
    mag = jnp.exp(lam_re * dt)
    a_re = mag * jnp.cos(lam_im * dt)
    a_im = mag * jnp.sin(lam_im * dt)
    inv_den = 1.0 / (lam_re * lam_re + lam_im * lam_im)
    coef_re = ((a_re - 1.0) * lam_re + a_im * lam_im) * inv_den
    coef_im = (a_im * lam_re - (a_re - 1.0) * lam_im) * inv_den
    bb_re = coef_re[..., None] * b_re - coef_im[..., None] * b_im
    bb_im = coef_re[..., None] * b_im + coef_im[..., None] * b_re
    pw_re, pw_im = [jnp.ones_like(a_re)], [jnp.zeros_like(a_im)]
    for _ in range(S5_L):
        pw_re, pw_im = (pw_re + [pw_re[-1] * a_re - pw_im[-1] * a_im],
                        pw_im + [pw_re[-1] * a_im + pw_im[-1] * a_re])
    pw_re, pw_im = jnp.stack(pw_re), jnp.stack(pw_im)
    ca_re = c_re[None] * pw_re[:, :, None, :] - c_im[None] * pw_im[:, :, None, :]
    ca_im = c_re[None] * pw_im[:, :, None, :] + c_im[None] * pw_re[:, :, None, :]
    kern = (jnp.einsum('kgop,gpi->kgio', ca_re[:S5_L], bb_re)
            - jnp.einsum('kgop,gpi->kgio', ca_im[:S5_L], bb_im))
    k_c = jnp.transpose(kern, (1, 2, 0, 3)).astype(BF16).reshape(
        S5_PAIRS, 2, SSM_GROUP, LANES)
    rev = S5_L - 1 - np.arange(S5_L)
    rev_re, rev_im = pw_re[rev][..., None], pw_im[rev][..., None]
    ab = jnp.stack([rev_re * bb_re[None] - rev_im * bb_im[None],
                    rev_re * bb_im[None] + rev_im * bb_re[None]])
    a_c = jnp.transpose(ab, (1, 2, 4, 0, 3)).astype(BF16).reshape(
        S5_L, S5_PAIRS, 2, SSM_GROUP, LANES)
    co = jnp.stack([ca_re[1:], -ca_im[1:]])
    c_c = jnp.transpose(co, (0, 2, 4, 1, 3)).astype(BF16).reshape(
        2, S5_PAIRS, 2, SSM_STATE, LANES)
    wt, wi, wo = _s5_weights(k_c, a_c, c_c)

    src = np.arange(S5_SUPER).reshape(4, 4, 2 * SSM_GROUP)
    dst = np.transpose(src, (1, 0, 2)).reshape(-1)
    perm = jnp.asarray(dst[None, :] == np.arange(S5_SUPER)[:, None], BF16)

    def bcast(a):
        return jnp.broadcast_to(a.reshape(S5_PAIRS, 1, LANES), (S5_PAIRS, SUBLANES, LANES))

    return perm, perm.T, wi, wt, wo, bcast(pw_re[S5_L]), bcast(pw_im[S5_L])


def _s5_placements():
    c, p, n = SSM_GROUP, SSM_STATE, S5_L
    shift = np.zeros((n, 2, n, c, n, 2, c), np.float32)
    place_y = np.zeros((2, n, c, n, 2, c), np.float32)
    place_s = np.zeros((2, 2, p, 2, 2, p), np.float32)
    for g in range(2):
        for k in range(n):
            place_y[g, k, :, k, g, :] = np.eye(c)
            for j in range(k + 1):
                shift[j, g, k - j, :, k, g, :] = np.eye(c)
        for part in range(2):
            place_s[g, part, :, part, g, :] = np.eye(p)
    as_bf16 = lambda m, lead: jnp.asarray(m.reshape(lead + (LANES, 2 * LANES)), BF16)
    return as_bf16(shift, (n, 2)), as_bf16(place_s, (2,)), as_bf16(place_y, (2,))


def _s5_weights_kernel(k_ref, a_ref, c_ref, shift_ref, place_s_ref, place_y_ref, wt_ref, wi_ref, wo_ref):
    n_pairs = k_ref.shape[0]
    for g in range(2):
        k_g = k_ref[:, g].reshape(n_pairs * SSM_GROUP, LANES)
        for j in range(S5_L):
            wt_ref[j, :, g] = jnp.dot(k_g, shift_ref[j, g], preferred_element_type=F32).astype(
                wt_ref.dtype).reshape(n_pairs, SSM_GROUP, 2 * LANES)
            a_jg = a_ref[j, :, g].reshape(n_pairs * SSM_GROUP, LANES)
            wi_ref[j, :, g] = jnp.dot(a_jg, place_s_ref[g], preferred_element_type=F32).astype(
                wi_ref.dtype).reshape(n_pairs, SSM_GROUP, 2 * LANES)
        for part in range(2):
            c_pg = c_ref[part, :, g].reshape(n_pairs * SSM_STATE, LANES)
            wo_ref[part, :, g] = jnp.dot(c_pg, place_y_ref[g], preferred_element_type=F32).astype(
                wo_ref.dtype).reshape(n_pairs, SSM_STATE, 2 * LANES)


def _s5_weights(k_c, a_c, c_c):
    wide = 2 * LANES
    wt, wi, wo = pl.pallas_call(
        _s5_weights_kernel,
        out_shape=[jax.ShapeDtypeStruct((S5_L, S5_PAIRS, 2, SSM_GROUP, wide), BF16),
                   jax.ShapeDtypeStruct((S5_L, S5_PAIRS, 2, SSM_GROUP, wide), BF16),
                   jax.ShapeDtypeStruct((2, S5_PAIRS, 2, SSM_STATE, wide), BF16)],
        compiler_params=pltpu.CompilerParams(vmem_limit_bytes=VMEM_LIMIT_BYTES),
        name="s5_weights",
    )(k_c, a_c, c_c, *_s5_placements())
    return (wt.reshape(S5_L, S5_PAIRS, 2 * SSM_GROUP, wide), wi.reshape(S5_L, S5_PAIRS, 2 * SSM_GROUP, wide),
            wo.reshape(2, S5_PAIRS, 2 * SSM_STATE, wide))


ROW_SUBTILES = 2
MLP_CHUNKS = D_FF // D_MODEL


def _skewed(n_stages, stage):
    for step in range(n_stages + ROW_SUBTILES - 1):
        for r in range(ROW_SUBTILES):
            k = step - r
            if 0 <= k < n_stages:
                stage(r, k)


def _sub_rows(ref, r):
    sub = ref.shape[0] // ROW_SUBTILES
    return slice(r * sub, (r + 1) * sub)


def _mlp_stages(d, k, rows, g1_ref, b1_ref, w1_ref, w2_ref, g2_ref, b2_ref, o_ref):
    if k == 0:
        h = _layer_norm(d.pop("pre"), g1_ref[...], b1_ref[...])
        d["hb"] = h.astype(BF16)
        d["acc"] = DN_ALPHA * h
    elif k <= MLP_CHUNKS:
        cols = slice((k - 1) * D_MODEL, k * D_MODEL)
        u = jnp.maximum(jnp.dot(d["hb"], w1_ref[:, cols], preferred_element_type=F32), 0.0)
        d["acc"] = d["acc"] + jnp.dot((u * u).astype(BF16), w2_ref[cols, :],
                                      preferred_element_type=F32)
    else:
        d["out"] = _layer_norm(d.pop("acc"), g2_ref[...], b2_ref[...])
        o_ref[rows, :] = d["out"]


def _s5_tail_mlp_proj_kernel(y_ref, x_ref, pos_ref, wg_ref, wo_ref, g1_ref, b1_ref, w1_ref, w2_ref,
                             g2_ref, b2_ref, invf_ref, wa_ref, kvg_ref, kvwb_ref, qg_ref, qwb_ref,
                             o_ref, q_ref, kn_ref, kr_ref, v_ref):
    state = [dict() for _ in range(ROW_SUBTILES)]
    n_mlp = MLP_CHUNKS + 2

    def stage(r, k):
        d, rows = state[r], _sub_rows(o_ref, r)
        if k == 0:
            d["z"] = jnp.dot(y_ref[rows, :], wg_ref[...], preferred_element_type=F32)
        elif k == 1:
            z = d.pop("z")
            m = (z[:, :D_MODEL] * jax.nn.sigmoid(z[:, D_MODEL:])).astype(BF16)
            d["pre"] = DN_ALPHA * x_ref[rows, :] + jnp.dot(m, wo_ref[...],
                                                           preferred_element_type=F32)
        elif k < 2 + n_mlp:
            _mlp_stages(d, k - 2, rows, g1_ref, b1_ref, w1_ref, w2_ref, g2_ref, b2_ref, o_ref)
        else:
            _proj_stages(d, k - 2 - n_mlp, rows, pos_ref, invf_ref, wa_ref, kvg_ref, kvwb_ref,
                         qg_ref, qwb_ref, q_ref, kn_ref, kr_ref, v_ref)

    _skewed(2 + n_mlp + PROJ_STAGES, stage)


def _attn_out_mlp_kernel(a_ref, r_ref, wo_ref, g1_ref, b1_ref, w1_ref, w2_ref, g2_ref, b2_ref, o_ref):
    state = [dict() for _ in range(ROW_SUBTILES)]

    def stage(r, k):
        d, rows = state[r], _sub_rows(o_ref, r)
        if k == 0:
            d["pre"] = DN_ALPHA * r_ref[rows, :] + jnp.dot(a_ref[rows, :], wo_ref[...],
                                                           preferred_element_type=F32)
        else:
            _mlp_stages(d, k - 1, rows, g1_ref, b1_ref, w1_ref, w2_ref, g2_ref, b2_ref, o_ref)

    _skewed(MLP_CHUNKS + 3, stage)


def _row_stage(body, name, row_inputs, const_inputs, w1_all, w2_all, layer, ln2, tm=ROW_TILE):
    m = row_inputs[0].shape[0]
    row = lambda i: (i, 0)
    layer_spec = lambda w: pl.BlockSpec((None,) + w.shape[1:], lambda i: (layer, 0, 0),
                                        pipeline_mode=pl.Buffered(1))
    return pl.pallas_call(
        body,
        grid=(m // tm,),
        in_specs=([pl.BlockSpec((tm, a.shape[1]), row) for a in row_inputs]
                  + [_const_spec(c.shape) for c in const_inputs]
                  + [layer_spec(w1_all), layer_spec(w2_all)]
                  + [_const_spec(c.shape) for c in ln2]),
        out_specs=pl.BlockSpec((tm, D_MODEL), row),
        out_shape=jax.ShapeDtypeStruct((m, D_MODEL), F32),
        compiler_params=pltpu.CompilerParams(
            dimension_semantics=("parallel",), vmem_limit_bytes=VMEM_LIMIT_BYTES),
        name=name,
    )(*row_inputs, *const_inputs, w1_all, w2_all, *ln2)


ROPE_HALF = QK_ROPE // 2
PROJ_A_COLS = 3 * MXU_DIM
Q_ROPE_TILES = N_HEADS * QK_ROPE // LANES


PROJ_STAGES = 4


def _proj_stages(d, k, rows, pos_ref, invf_ref, wa_ref, kvg_ref, kvwb_ref, qg_ref, qwb_ref,
                 q_ref, kn_ref, kr_ref, v_ref):
    c_q0 = KV_LORA
    k_rope0 = KV_LORA + Q_LORA

    def rope(x):
        swapped = jnp.where(d["first"], pltpu.roll(x, LANES - ROPE_HALF, axis=1),
                            pltpu.roll(x, ROPE_HALF, axis=1))
        return x * d["cos"] + swapped * d["sin"]

    if k == 0:
        d["a"] = jnp.dot(d.pop("out").astype(BF16), wa_ref[...], preferred_element_type=F32)
        ang = pos_ref[rows, :] * invf_ref[...]
        lane = lax.broadcasted_iota(jnp.int32, ang.shape, 1)
        d["first"] = (lane & ROPE_HALF) == 0
        d["low"] = lane < QK_ROPE
        d["cos"] = jnp.cos(ang)
        d["sin"] = jnp.sin(ang)
    elif k == 1:
        a = d.pop("a")
        kr_ref[rows, :] = rope(a[:, k_rope0:k_rope0 + LANES]).astype(kr_ref.dtype)
        c_kv = _rms_norm(a[:, :KV_LORA], kvg_ref[...]).astype(BF16)
        d["c_q"] = _rms_norm(a[:, c_q0:c_q0 + Q_LORA], qg_ref[...]).astype(BF16)
        d["kv"] = jnp.dot(c_kv, kvwb_ref[...], preferred_element_type=F32)
    elif k == 2:
        kv = d.pop("kv")
        for hd in range(N_HEADS):
            kn_ref[hd, rows, :] = kv[:, hd * QK_NOPE:(hd + 1) * QK_NOPE].astype(kn_ref.dtype)
            v0 = N_HEADS * QK_NOPE + hd * V_HEAD
            v_ref[hd, rows, :] = kv[:, v0:v0 + V_HEAD].astype(v_ref.dtype)
        d["q"] = jnp.dot(d.pop("c_q"), qwb_ref[...], preferred_element_type=F32) * SM_SCALE
    else:
        q = d.pop("q")
        for hd in range(N_HEADS):
            q_ref[hd, rows, :LANES] = q[:, hd * LANES:(hd + 1) * LANES].astype(q_ref.dtype)
        for p in range(Q_ROPE_TILES):
            c0 = N_HEADS * QK_NOPE + p * LANES
            y = rope(q[:, c0:c0 + LANES])
            for hd, y_hd in ((2 * p, y), (2 * p + 1, pltpu.roll(y, QK_ROPE, axis=1))):
                q_ref[hd, rows, LANES:] = jnp.where(d["low"], y_hd, 0.0).astype(q_ref.dtype)


def _s5_tail_mlp_proj(y, x, pos, tail_consts, w1_all, w2_all, layer, ln2, proj_consts, nb, seq,
                      tm=ROW_TILE):
    nt = seq // tm
    row = lambda b, i: (b * nt + i, 0)
    head = lambda b, i: (b, 0, i, 0)
    layer_spec = lambda w: pl.BlockSpec((None,) + w.shape[1:], lambda b, i: (layer, 0, 0),
                                        pipeline_mode=pl.Buffered(1))
    consts = lambda cs: [_const_spec(c.shape) for c in cs]
    return pl.pallas_call(
        _s5_tail_mlp_proj_kernel,
        grid=(nb, nt),
        in_specs=([pl.BlockSpec((tm, D_MODEL), row), pl.BlockSpec((tm, D_MODEL), row),
                   pl.BlockSpec((tm, 1), row)]
                  + consts(tail_consts) + [layer_spec(w1_all), layer_spec(w2_all)]
                  + consts(ln2) + consts(proj_consts)),
        out_specs=[pl.BlockSpec((tm, D_MODEL), row),
                   pl.BlockSpec((None, N_HEADS, tm, 2 * LANES), head),
                   pl.BlockSpec((None, N_HEADS, tm, QK_NOPE), head),
                   pl.BlockSpec((None, tm, LANES), lambda b, i: (b, i, 0)),
                   pl.BlockSpec((None, N_HEADS, tm, V_HEAD), head)],
        out_shape=[jax.ShapeDtypeStruct((nb * seq, D_MODEL), F32),
                   jax.ShapeDtypeStruct((nb, N_HEADS, seq, 2 * LANES), BF16),
                   jax.ShapeDtypeStruct((nb, N_HEADS, seq, QK_NOPE), BF16),
                   jax.ShapeDtypeStruct((nb, seq, LANES), BF16),
                   jax.ShapeDtypeStruct((nb, N_HEADS, seq, V_HEAD), BF16)],
        compiler_params=pltpu.CompilerParams(
            dimension_semantics=("parallel", "parallel"), vmem_limit_bytes=VMEM_LIMIT_FUSED_BYTES),
        name="s5_tail_mlp_proj",
    )(y, x, pos, *tail_consts, w1_all, w2_all, *ln2, *proj_consts)


def _mla_weights(kv_w_a, kv_w_b, q_w_a, q_w_b):
    pad = jnp.zeros((D_MODEL, PROJ_A_COLS - KV_LORA - Q_LORA - QK_ROPE), F32)
    w_a = jnp.concatenate([kv_w_a[:, :KV_LORA], q_w_a, kv_w_a[:, KV_LORA:], pad], axis=1)
    kv_wb3 = kv_w_b.reshape(KV_LORA, N_HEADS, QK_NOPE + V_HEAD)
    kv_wb = jnp.concatenate([kv_wb3[:, :, :QK_NOPE].reshape(KV_LORA, -1),
                             kv_wb3[:, :, QK_NOPE:].reshape(KV_LORA, -1)], axis=1)
    q_wb3 = q_w_b.reshape(Q_LORA, N_HEADS, QK_NOPE + QK_ROPE)
    q_wb = jnp.concatenate([q_wb3[:, :, :QK_NOPE].reshape(Q_LORA, -1),
                            q_wb3[:, :, QK_NOPE:].reshape(Q_LORA, -1)], axis=1)
    return w_a.astype(BF16), kv_wb.astype(BF16), q_wb.astype(BF16)


def _attn_kernel(q_ref, kn_ref, kr_ref, v_ref, o_ref, s_ref, m_ref, acc_ref):
    tq = ATTN_TQ
    half = tq // 2
    n_q = q_ref.shape[0] // tq
    blocks = []
    for qi in range(n_q):
        k_end = (qi + 1) * tq
        for k0 in range(0, k_end, ATTN_TK):
            nk = min(ATTN_TK, k_end - k0)
            blocks.append((qi, k0, nk, k0 + nk == k_end))
    nt = (((1,), (1,)), ((), ()))
    ones = jnp.ones((ATTN_TK, LANES), BF16)

    def pieces(nkeys, diag):
        return ((0, half, nkeys - half), (half, tq, nkeys)) if diag else ((0, tq, nkeys),)

    def scores(n):
        qi, k0, nkeys, diag = blocks[n]
        k = jnp.concatenate([kn_ref[k0:k0 + nkeys, :], kr_ref[k0:k0 + nkeys, :]], axis=1)
        for r0, r1, nk in pieces(nkeys, diag):
            s_ref[n % 2, r0:r1, :nk] = lax.dot_general(
                q_ref[qi * tq + r0:qi * tq + r1, :], k[:nk], nt, preferred_element_type=F32)

    def consume(n):
        qi, k0, nkeys, diag = blocks[n]
        buf = qi % 2
        for r0, r1, nk in pieces(nkeys, diag):
            s = s_ref[n % 2, r0:r1, :nk]
            if diag:
                sq = s[:, nk - half:]
                keep = (lax.broadcasted_iota(jnp.int32, sq.shape, 1)
                        <= lax.broadcasted_iota(jnp.int32, sq.shape, 0))
                sq = jnp.where(keep, sq, NEG_INF)
                s = sq if nk == half else jnp.concatenate([s[:, :nk - half], sq], axis=1)
            m_new = jnp.max(s, axis=-1, keepdims=True)
            if k0 == 0:
                m_new = jnp.broadcast_to(m_new, (r1 - r0, LANES))
            else:
                m_old = m_ref[buf, r0:r1, :]
                m_new = jnp.maximum(m_old, m_new)
            p = jnp.exp((s - jnp.concatenate([m_new] * (nk // LANES), axis=1)).astype(BF16))
            v_ext = jnp.concatenate([v_ref[k0:k0 + nk, :], ones[:nk]], axis=1)
            acc = jnp.dot(p, v_ext, preferred_element_type=F32)
            if k0 > 0:
                alpha = jnp.exp(m_old - m_new)
                acc = jnp.concatenate([alpha, alpha], axis=1) * acc_ref[buf, r0:r1, :] + acc
            if diag:
                o_ref[qi * tq + r0:qi * tq + r1, :] = (
                    acc[:, :V_HEAD] / acc[:, V_HEAD:]).astype(o_ref.dtype)
            else:
                acc_ref[buf, r0:r1, :] = acc
                m_ref[buf, r0:r1, :] = m_new

    scores(0)
    for n in range(len(blocks)):
        if n + 1 < len(blocks):
            scores(n + 1)
        consume(n)


def _attention(q, kn, kr, v):
    nb, nh, seq, _ = q.shape
    assert seq % ATTN_TQ == 0
    return pl.pallas_call(
        _attn_kernel,
        grid=(nb, nh),
        in_specs=[pl.BlockSpec((None, None, seq, 2 * LANES), lambda b, h: (b, h, 0, 0)),
                  pl.BlockSpec((None, None, seq, QK_NOPE), lambda b, h: (b, h, 0, 0)),
                  pl.BlockSpec((None, seq, LANES), lambda b, h: (b, 0, 0)),
                  pl.BlockSpec((None, None, seq, V_HEAD), lambda b, h: (b, h, 0, 0))],
        out_specs=pl.BlockSpec((None, seq, V_HEAD), lambda b, h: (b, 0, h)),
        out_shape=jax.ShapeDtypeStruct((nb, seq, nh * V_HEAD), BF16),
        scratch_shapes=[pltpu.VMEM((2, ATTN_TQ, ATTN_TK), F32),
                        pltpu.VMEM((2, ATTN_TQ, LANES), F32),
                        pltpu.VMEM((2, ATTN_TQ, V_HEAD + LANES), F32)],
        compiler_params=pltpu.CompilerParams(
            dimension_semantics=("parallel", "parallel"), vmem_limit_bytes=VMEM_LIMIT_BYTES),
        name="attention",
    )(q, kn, kr, v)


def _rope_inv_freq():
    inv_freq = ROPE_THETA ** (-jnp.arange(ROPE_HALF, dtype=F32) / ROPE_HALF)
    group = jnp.concatenate([-inv_freq, inv_freq])
    return jnp.tile(group, LANES // QK_ROPE).reshape(1, LANES)


def _row(v):
    return v.reshape(1, -1).astype(F32)


def kernel(x, positions, ln_mix_g, ln_mix_b, ln_ffn_g, ln_ffn_b, w_ff1, w_ff2, ssm_lam_re, ssm_lam_im, ssm_log_dt, ssm_b_re, ssm_b_im, ssm_c_re, ssm_c_im, ssm_d, ssm_w_glu, ssm_w_out, kv_w_a, kv_norm_g, kv_w_b, q_w_a, q_norm_g, q_w_b, attn_w_o):
    nb, seq, _ = x.shape
    m = nb * seq
    pos = positions.astype(F32).reshape(m, 1)
    inv_freq = _rope_inv_freq()
    w1_all = w_ff1.astype(BF16)
    w2_all = w_ff2.astype(BF16)

    assert N_A_LAYERS == 1 and DEPTH == 2
    h = x.reshape(m, D_MODEL)
    ln1 = [(_row(ln_mix_g[k]), _row(ln_mix_b[k])) for k in range(DEPTH)]
    ln2 = [(_row(ln_ffn_g[k]), _row(ln_ffn_b[k])) for k in range(DEPTH)]

    s5_params = _s5_params(ssm_lam_re[0], ssm_lam_im[0], ssm_log_dt[0],
                           ssm_b_re[0], ssm_b_im[0], ssm_c_re[0], ssm_c_im[0])
    y = _s5_scan(h.reshape(nb, seq, D_MODEL), s5_params, _row(ssm_d[0]))
    w_a, kv_wb, q_wb = _mla_weights(kv_w_a, kv_w_b, q_w_a[0], q_w_b[0])
    h, q, kn, kr, v = _s5_tail_mlp_proj(
        y.reshape(m, D_MODEL), h, pos,
        (ssm_w_glu[0].astype(BF16), ssm_w_out[0].astype(BF16)) + ln1[0], w1_all, w2_all, 0, ln2[0],
        (inv_freq, w_a, _row(kv_norm_g), kv_wb, _row(q_norm_g[0]), q_wb), nb, seq)
    o = _attention(q, kn, kr, v)
    h = _row_stage(_attn_out_mlp_kernel, "attn_out_mlp", (o.reshape(m, N_HEADS * V_HEAD), h),
                   (attn_w_o[0].astype(BF16),) + ln1[1], w1_all, w2_all, 1, ln2[1], tm=2 * ROW_TILE)
    return h.reshape(nb, seq, D_MODEL)
```

```python
import math

import jax
import numpy as np
import jax.numpy as jnp
from jax import lax
from jax.experimental import pallas as pl
from jax.experimental.pallas import tpu as pltpu

F32 = jnp.float32
BF16 = jnp.bfloat16

D_MODEL = 1024
DEPTH = 2
SSM_GROUP = 16
N_GROUPS = D_MODEL // SSM_GROUP
SSM_STATE = 64
N_HEADS = 8
QK_NOPE = 128
QK_ROPE = 64
V_HEAD = 128
Q_LORA = 384
KV_LORA = 256
ROPE_THETA = 10000.0
SM_SCALE = (QK_NOPE + QK_ROPE) ** -0.5
Q_SCALE = SM_SCALE * math.log2(math.e)
NEG_INF = -1e30
D_FF = 4 * D_MODEL
N_A_LAYERS = DEPTH // 2
DN_ALPHA = (2 * DEPTH) ** 0.25
LN_EPS = 1e-5
RMS_EPS = 1e-6

LANES = 128
SUBLANES = 8
MXU_DIM = 256
VMEM_LIMIT_BYTES = 56 * 1024 * 1024
VMEM_LIMIT_FUSED_BYTES = 58 * 1024 * 1024

S5_L = 8
S5_T = 256
S5_PAIRS = N_GROUPS // 2
S5_N_SLABS = D_MODEL // LANES
S5_SUPER = 4 * LANES
S5_PITCH = S5_T + SUBLANES

ROW_TILE = 512
ATTN_TQ = 512
ATTN_TK = 1024


def _layer_norm(x, g, b):
    mu = jnp.mean(x, axis=-1, keepdims=True)
    xc = x - mu
    var = jnp.mean(xc * xc, axis=-1, keepdims=True)
    return xc * lax.rsqrt(var + LN_EPS) * g + b


def _rms_norm(x, g):
    return x * lax.rsqrt(jnp.mean(x * x, axis=-1, keepdims=True) + RMS_EPS) * g


def _gelu_tanh(x):
    c = math.sqrt(2.0 / math.pi)
    half_x = 0.5 * x
    return half_x + half_x * jnp.tanh(x * (c + (0.044715 * c) * (x * x)))


def _const_spec(shape):
    nd = len(shape)
    return pl.BlockSpec(shape, lambda *_: (0,) * nd, pipeline_mode=pl.Buffered(1))


def _s5_kernel(x_ref, perm_ref, permt_ref, wi_ref, wt_ref, wo_ref, a8r_ref, a8i_ref, d_ref, o_ref,
               xs_ref, hc_ref):
    nb, t_len, _ = x_ref.shape
    n_chunks = t_len // S5_L
    halves = S5_L * LANES // S5_SUPER

    @pl.when(pl.program_id(0) == 0)
    def _():
        hc_ref[...] = jnp.zeros_like(hc_ref)

    for b in range(nb):
        x_b = x_ref[b]
        for l in range(S5_N_SLABS):
            xs_ref[l, pl.ds(b * S5_PITCH, t_len), :] = x_b[:, l * LANES:(l + 1) * LANES]

    def step_rows(c, j):
        return pl.ds(S5_L * c + j, nb, stride=S5_PITCH)

    def gather(l):
        z_half = []
        for a in range(halves):
            lhs = jnp.concatenate(
                [jnp.concatenate([xs_ref[l, step_rows(c, 4 * a + jj), :] for c in range(n_chunks)],
                                 axis=0) for jj in range(4)], axis=1).astype(BF16)
            z_half.append(jnp.dot(lhs, perm_ref[...], preferred_element_type=F32).astype(BF16))
        return z_half

    def mix(l, z_half):
        y_pair = []
        for gh in range(4):
            q = 4 * l + gh
            z_q = jnp.concatenate([zh[:, gh * LANES:(gh + 1) * LANES] for zh in z_half], axis=1)
            w_in = jnp.concatenate([wi_ref[j, q] for j in range(S5_L)], axis=0)
            w_y = jnp.concatenate([wt_ref[j, q] for j in range(S5_L)] + [wo_ref[0, q], wo_ref[1, q]],
                                  axis=0)
            s_q = jnp.dot(z_q, w_in, preferred_element_type=F32)
            a_re, a_im = a8r_ref[q], a8i_ref[q]
            h_re, h_im = hc_ref[2 * q], hc_ref[2 * q + 1]
            ent_re, ent_im = [], []
            for c in range(n_chunks):
                ent_re.append(h_re)
                ent_im.append(h_im)
                rows_c = slice(c * nb, (c + 1) * nb)
                n_re = a_re * h_re - a_im * h_im + s_q[rows_c, :LANES]
                n_im = a_re * h_im + a_im * h_re + s_q[rows_c, LANES:]
                h_re, h_im = n_re, n_im
            hc_ref[2 * q] = h_re
            hc_ref[2 * q + 1] = h_im
            ent = jnp.concatenate([jnp.concatenate(ent_re, axis=0),
                                   jnp.concatenate(ent_im, axis=0)], axis=1).astype(BF16)
            y_pair.append(jnp.dot(jnp.concatenate([z_q, ent], axis=1), w_y,
                                  preferred_element_type=F32))
        return y_pair

    def emit(l, y_pair):
        d_l = d_ref[:, l * LANES:(l + 1) * LANES]
        for a in range(halves):
            y_l = jnp.concatenate([y[:, a * LANES:(a + 1) * LANES] for y in y_pair], axis=1)
            out = jnp.dot(y_l.astype(BF16), permt_ref[...], preferred_element_type=F32)
            for jj in range(4):
                for c in range(n_chunks):
                    rows = step_rows(c, 4 * a + jj)
                    xs_ref[l, rows, :] = (out[c * nb:(c + 1) * nb, jj * LANES:(jj + 1) * LANES]
                                          + d_l * xs_ref[l, rows, :])
        for b in range(nb):
            o_ref[b, :, l * LANES:(l + 1) * LANES] = _gelu_tanh(
                xs_ref[l, pl.ds(b * S5_PITCH, t_len), :]).astype(o_ref.dtype)

    z_next = gather(0)
    for l in range(S5_N_SLABS):
        y_pair = mix(l, z_next)
        if l + 1 < S5_N_SLABS:
            z_next = gather(l + 1)
        emit(l, y_pair)


def _s5_scan(x, params, d_skip, t_len=S5_T):
    nb, seq, _ = x.shape
    assert nb == SUBLANES and seq % t_len == 0 and t_len % S5_L == 0
    return pl.pallas_call(
        _s5_kernel,
        grid=(seq // t_len,),
        in_specs=[pl.BlockSpec((nb, t_len, D_MODEL), lambda i: (0, i, 0))]
        + [_const_spec(p.shape) for p in params] + [_const_spec(d_skip.shape)],
        out_specs=pl.BlockSpec((nb, t_len, D_MODEL), lambda i: (0, i, 0)),
        out_shape=jax.ShapeDtypeStruct(x.shape, BF16),
        scratch_shapes=[
            pltpu.VMEM((S5_N_SLABS, nb * S5_PITCH, LANES), F32),
            pltpu.VMEM((N_GROUPS, SUBLANES, LANES), F32),
        ],
        compiler_params=pltpu.CompilerParams(
            dimension_semantics=("arbitrary",), vmem_limit_bytes=VMEM_LIMIT_BYTES),
        name="s5_scan",
    )(x, *params, d_skip)


def _s5_params(lam_re, lam_im, log_dt, b_re, b_im, c_re, c_im):
    dt = jnp.exp(log_dt)[:, None]
    mag = jnp.exp(lam_re * dt)
    a_re = mag * jnp.cos(lam_im * dt)
    a_im = mag * jnp.sin(lam_im * dt)
    inv_den = 1.0 / (lam_re * lam_re + lam_im * lam_im)
    coef_re = ((a_re - 1.0) * lam_re + a_im * lam_im) * inv_den
    coef_im = (a_im * lam_re - (a_re - 1.0) * lam_im) * inv_den
    bb_re = coef_re[..., None] * b_re - coef_im[..., None] * b_im
    bb_im = coef_re[..., None] * b_im + coef_im[..., None] * b_re
    pw_re, pw_im = [jnp.ones_like(a_re)], [jnp.zeros_like(a_im)]
    for _ in range(S5_L):
        pw_re, pw_im = (pw_re + [pw_re[-1] * a_re - pw_im[-1] * a_im],
                        pw_im + [pw_re[-1] * a_im + pw_im[-1] * a_re])
    pw_re, pw_im = jnp.stack(pw_re), jnp.stack(pw_im)
    ca_re = c_re[None] * pw_re[:, :, None, :] - c_im[None] * pw_im[:, :, None, :]
    ca_im = c_re[None] * pw_im[:, :, None, :] + c_im[None] * pw_re[:, :, None, :]
    kern = (jnp.einsum('kgop,gpi->kgio', ca_re[:S5_L], bb_re)
            - jnp.einsum('kgop,gpi->kgio', ca_im[:S5_L], bb_im))
    k_c = jnp.transpose(kern, (1, 2, 0, 3)).astype(BF16).reshape(
        S5_PAIRS, 2, SSM_GROUP, LANES)
    rev = S5_L - 1 - np.arange(S5_L)
    rev_re, rev_im = pw_re[rev][..., None], pw_im[rev][..., None]
    ab = jnp.stack([rev_re * bb_re[None] - rev_im * bb_im[None],
                    rev_re * bb_im[None] + rev_im * bb_re[None]])
    a_c = jnp.transpose(ab, (1, 2, 4, 0, 3)).astype(BF16).reshape(
        S5_L, S5_PAIRS, 2, SSM_GROUP, LANES)
    co = jnp.stack([ca_re[1:], -ca_im[1:]])
    c_c = jnp.transpose(co, (0, 2, 4, 1, 3)).astype(BF16).reshape(
        2, S5_PAIRS, 2, SSM_STATE, LANES)
    wt, wi, wo = _s5_weights(k_c, a_c, c_c)

    src = np.arange(S5_SUPER).reshape(4, 4, 2 * SSM_GROUP)
    dst = np.transpose(src, (1, 0, 2)).reshape(-1)
    perm = jnp.asarray(dst[None, :] == np.arange(S5_SUPER)[:, None], BF16)

    def bcast(a):
        return jnp.broadcast_to(a.reshape(S5_PAIRS, 1, LANES), (S5_PAIRS, SUBLANES, LANES))

    return perm, perm.T, wi, wt, wo, bcast(pw_re[S5_L]), bcast(pw_im[S5_L])


def _s5_placements():
    c, p, n = SSM_GROUP, SSM_STATE, S5_L
    shift = np.zeros((n, 2, n, c, n, 2, c), np.float32)
    place_y = np.zeros((2, n, c, n, 2, c), np.float32)
    place_s = np.zeros((2, 2, p, 2, 2, p), np.float32)
    for g in range(2):
        for k in range(n):
            place_y[g, k, :, k, g, :] = np.eye(c)
            for j in range(k + 1):
                shift[j, g, k - j, :, k, g, :] = np.eye(c)
        for part in range(2):
            place_s[g, part, :, part, g, :] = np.eye(p)
    as_bf16 = lambda m, lead: jnp.asarray(m.reshape(lead + (LANES, 2 * LANES)), BF16)
    return as_bf16(shift, (n, 2)), as_bf16(place_s, (2,)), as_bf16(place_y, (2,))


def _s5_weights_kernel(k_ref, a_ref, c_ref, shift_ref, place_s_ref, place_y_ref, wt_ref, wi_ref, wo_ref):
    n_pairs = k_ref.shape[0]
    for g in range(2):
        k_g = k_ref[:, g].reshape(n_pairs * SSM_GROUP, LANES)
        for j in range(S5_L):
            wt_ref[j, :, g] = jnp.dot(k_g, shift_ref[j, g], preferred_element_type=F32).astype(
                wt_ref.dtype).reshape(n_pairs, SSM_GROUP, 2 * LANES)
            a_jg = a_ref[j, :, g].reshape(n_pairs * SSM_GROUP, LANES)
            wi_ref[j, :, g] = jnp.dot(a_jg, place_s_ref[g], preferred_element_type=F32).astype(
                wi_ref.dtype).reshape(n_pairs, SSM_GROUP, 2 * LANES)
        for part in range(2):
            c_pg = c_ref[part, :, g].reshape(n_pairs * SSM_STATE, LANES)
            wo_ref[part, :, g] = jnp.dot(c_pg, place_y_ref[g], preferred_element_type=F32).astype(
                wo_ref.dtype).reshape(n_pairs, SSM_STATE, 2 * LANES)


def _s5_weights(k_c, a_c, c_c):
    wide = 2 * LANES
    wt, wi, wo = pl.pallas_call(
        _s5_weights_kernel,
        out_shape=[jax.ShapeDtypeStruct((S5_L, S5_PAIRS, 2, SSM_GROUP, wide), BF16),
                   jax.ShapeDtypeStruct((S5_L, S5_PAIRS, 2, SSM_GROUP, wide), BF16),
                   jax.ShapeDtypeStruct((2, S5_PAIRS, 2, SSM_STATE, wide), BF16)],
        compiler_params=pltpu.CompilerParams(vmem_limit_bytes=VMEM_LIMIT_BYTES),
        name="s5_weights",
    )(k_c, a_c, c_c, *_s5_placements())
    return (wt.reshape(S5_L, S5_PAIRS, 2 * SSM_GROUP, wide), wi.reshape(S5_L, S5_PAIRS, 2 * SSM_GROUP, wide),
            wo.reshape(2, S5_PAIRS, 2 * SSM_STATE, wide))


ROW_SUBTILES = 2
MLP_CHUNKS = D_FF // D_MODEL


def _skewed(n_stages, stage):
    for step in range(n_stages + ROW_SUBTILES - 1):
        for r in range(ROW_SUBTILES):
            k = step - r
            if 0 <= k < n_stages:
                stage(r, k)


def _sub_rows(ref, r):
    sub = ref.shape[0] // ROW_SUBTILES
    return slice(r * sub, (r + 1) * sub)


def _mlp_stages(d, k, rows, g1_ref, b1_ref, w1_ref, w2_ref, g2_ref, b2_ref, o_ref):
    if k == 0:
        h = _layer_norm(d.pop("pre"), g1_ref[...], b1_ref[...])
        d["hb"] = h.astype(BF16)
        d["acc"] = DN_ALPHA * h
    elif k <= MLP_CHUNKS:
        cols = slice((k - 1) * D_MODEL, k * D_MODEL)
        u = jnp.maximum(jnp.dot(d["hb"], w1_ref[:, cols], preferred_element_type=F32), 0.0)
        d["acc"] = d["acc"] + jnp.dot((u * u).astype(BF16), w2_ref[cols, :],
                                      preferred_element_type=F32)
    else:
        d["out"] = _layer_norm(d.pop("acc"), g2_ref[...], b2_ref[...])
        o_ref[rows, :] = d["out"]


def _s5_tail_mlp_proj_kernel(y_ref, x_ref, pos_ref, wg_ref, wo_ref, g1_ref, b1_ref, w1_ref, w2_ref,
                             g2_ref, b2_ref, invf_ref, wa_ref, kvg_ref, kvwb_ref, qg_ref, qwb_ref,
                             o_ref, q_ref, kn_ref, kr_ref, v_ref):
    state = [dict() for _ in range(ROW_SUBTILES)]
    n_mlp = MLP_CHUNKS + 2

    def stage(r, k):
        d, rows = state[r], _sub_rows(o_ref, r)
        if k == 0:
            d["z"] = jnp.dot(y_ref[rows, :], wg_ref[...], preferred_element_type=F32)
        elif k == 1:
            z = d.pop("z")
            m = (z[:, :D_MODEL] * jax.nn.sigmoid(z[:, D_MODEL:])).astype(BF16)
            d["pre"] = DN_ALPHA * x_ref[rows, :] + jnp.dot(m, wo_ref[...],
                                                           preferred_element_type=F32)
        elif k < 2 + n_mlp:
            _mlp_stages(d, k - 2, rows, g1_ref, b1_ref, w1_ref, w2_ref, g2_ref, b2_ref, o_ref)
        else:
            _proj_stages(d, k - 2 - n_mlp, rows, pos_ref, invf_ref, wa_ref, kvg_ref, kvwb_ref,
                         qg_ref, qwb_ref, q_ref, kn_ref, kr_ref, v_ref)

    _skewed(2 + n_mlp + PROJ_STAGES, stage)


def _attn_out_mlp_kernel(a_ref, r_ref, wo_ref, g1_ref, b1_ref, w1_ref, w2_ref, g2_ref, b2_ref, o_ref):
    state = [dict() for _ in range(ROW_SUBTILES)]

    def stage(r, k):
        d, rows = state[r], _sub_rows(o_ref, r)
        if k == 0:
            d["pre"] = DN_ALPHA * r_ref[rows, :] + jnp.dot(a_ref[rows, :], wo_ref[...],
                                                           preferred_element_type=F32)
        else:
            _mlp_stages(d, k - 1, rows, g1_ref, b1_ref, w1_ref, w2_ref, g2_ref, b2_ref, o_ref)

    _skewed(MLP_CHUNKS + 3, stage)


def _row_stage(body, name, row_inputs, const_inputs, w1_all, w2_all, layer, ln2, tm=ROW_TILE):
    m = row_inputs[0].shape[0]
    row = lambda i: (i, 0)
    layer_spec = lambda w: pl.BlockSpec((None,) + w.shape[1:], lambda i: (layer, 0, 0),
                                        pipeline_mode=pl.Buffered(1))
    return pl.pallas_call(
        body,
        grid=(m // tm,),
        in_specs=([pl.BlockSpec((tm, a.shape[1]), row) for a in row_inputs]
                  + [_const_spec(c.shape) for c in const_inputs]
                  + [layer_spec(w1_all), layer_spec(w2_all)]
                  + [_const_spec(c.shape) for c in ln2]),
        out_specs=pl.BlockSpec((tm, D_MODEL), row),
        out_shape=jax.ShapeDtypeStruct((m, D_MODEL), F32),
        compiler_params=pltpu.CompilerParams(
            dimension_semantics=("parallel",), vmem_limit_bytes=VMEM_LIMIT_BYTES),
        name=name,
    )(*row_inputs, *const_inputs, w1_all, w2_all, *ln2)


ROPE_HALF = QK_ROPE // 2
PROJ_A_COLS = 3 * MXU_DIM
Q_ROPE_TILES = N_HEADS * QK_ROPE // LANES


PROJ_STAGES = 4


def _proj_stages(d, k, rows, pos_ref, invf_ref, wa_ref, kvg_ref, kvwb_ref, qg_ref, qwb_ref,
                 q_ref, kn_ref, kr_ref, v_ref):
    c_q0 = KV_LORA
    k_rope0 = KV_LORA + Q_LORA

    def rope(x):
        swapped = jnp.where(d["first"], pltpu.roll(x, LANES - ROPE_HALF, axis=1),
                            pltpu.roll(x, ROPE_HALF, axis=1))
        return x * d["cos"] + swapped * d["sin"]

    if k == 0:
        d["a"] = jnp.dot(d.pop("out").astype(BF16), wa_ref[...], preferred_element_type=F32)
        ang = pos_ref[rows, :] * invf_ref[...]
        lane = lax.broadcasted_iota(jnp.int32, ang.shape, 1)
        d["first"] = (lane & ROPE_HALF) == 0
        d["low"] = lane < QK_ROPE
        d["cos"] = jnp.cos(ang)
        d["sin"] = jnp.sin(ang)
    elif k == 1:
        a = d.pop("a")
        kr_ref[rows, :] = rope(a[:, k_rope0:k_rope0 + LANES]).astype(kr_ref.dtype)
        c_kv = _rms_norm(a[:, :KV_LORA], kvg_ref[...]).astype(BF16)
        d["c_q"] = _rms_norm(a[:, c_q0:c_q0 + Q_LORA], qg_ref[...]).astype(BF16)
        d["kv"] = jnp.dot(c_kv, kvwb_ref[...], preferred_element_type=F32)
    elif k == 2:
        kv = d.pop("kv")
        for hd in range(N_HEADS):
            kn_ref[hd, rows, :] = kv[:, hd * QK_NOPE:(hd + 1) * QK_NOPE].astype(kn_ref.dtype)
            v0 = N_HEADS * QK_NOPE + hd * V_HEAD
            v_ref[hd, rows, :] = kv[:, v0:v0 + V_HEAD].astype(v_ref.dtype)
        d["q"] = jnp.dot(d.pop("c_q"), qwb_ref[...], preferred_element_type=F32) * Q_SCALE
    else:
        q = d.pop("q")
        for hd in range(N_HEADS):
            q_ref[hd, rows, :LANES] = q[:, hd * LANES:(hd + 1) * LANES].astype(q_ref.dtype)
        for p in range(Q_ROPE_TILES):
            c0 = N_HEADS * QK_NOPE + p * LANES
            y = rope(q[:, c0:c0 + LANES])
            for hd, y_hd in ((2 * p, y), (2 * p + 1, pltpu.roll(y, QK_ROPE, axis=1))):
                q_ref[hd, rows, LANES:] = jnp.where(d["low"], y_hd, 0.0).astype(q_ref.dtype)


def _s5_tail_mlp_proj(y, x, pos, tail_consts, w1_all, w2_all, layer, ln2, proj_consts, nb, seq,
                      tm=ROW_TILE):
    nt = seq // tm
    row = lambda b, i: (b * nt + i, 0)
    head = lambda b, i: (b, 0, i, 0)
    layer_spec = lambda w: pl.BlockSpec((None,) + w.shape[1:], lambda b, i: (layer, 0, 0),
                                        pipeline_mode=pl.Buffered(1))
    consts = lambda cs: [_const_spec(c.shape) for c in cs]
    return pl.pallas_call(
        _s5_tail_mlp_proj_kernel,
        grid=(nb, nt),
        in_specs=([pl.BlockSpec((tm, D_MODEL), row), pl.BlockSpec((tm, D_MODEL), row),
                   pl.BlockSpec((tm, 1), row)]
                  + consts(tail_consts) + [layer_spec(w1_all), layer_spec(w2_all)]
                  + consts(ln2) + consts(proj_consts)),
        out_specs=[pl.BlockSpec((tm, D_MODEL), row),
                   pl.BlockSpec((None, N_HEADS, tm, 2 * LANES), head),
                   pl.BlockSpec((None, N_HEADS, tm, QK_NOPE), head),
                   pl.BlockSpec((None, tm, LANES), lambda b, i: (b, i, 0)),
                   pl.BlockSpec((None, N_HEADS, tm, V_HEAD), head)],
        out_shape=[jax.ShapeDtypeStruct((nb * seq, D_MODEL), F32),
                   jax.ShapeDtypeStruct((nb, N_HEADS, seq, 2 * LANES), BF16),
                   jax.ShapeDtypeStruct((nb, N_HEADS, seq, QK_NOPE), BF16),
                   jax.ShapeDtypeStruct((nb, seq, LANES), BF16),
                   jax.ShapeDtypeStruct((nb, N_HEADS, seq, V_HEAD), BF16)],
        compiler_params=pltpu.CompilerParams(
            dimension_semantics=("parallel", "parallel"), vmem_limit_bytes=VMEM_LIMIT_FUSED_BYTES),
        name="s5_tail_mlp_proj",
    )(y, x, pos, *tail_consts, w1_all, w2_all, *ln2, *proj_consts)


def _mla_weights(kv_w_a, kv_w_b, q_w_a, q_w_b):
    pad = jnp.zeros((D_MODEL, PROJ_A_COLS - KV_LORA - Q_LORA - QK_ROPE), F32)
    w_a = jnp.concatenate([kv_w_a[:, :KV_LORA], q_w_a, kv_w_a[:, KV_LORA:], pad], axis=1)
    kv_wb3 = kv_w_b.reshape(KV_LORA, N_HEADS, QK_NOPE + V_HEAD)
    kv_wb = jnp.concatenate([kv_wb3[:, :, :QK_NOPE].reshape(KV_LORA, -1),
                             kv_wb3[:, :, QK_NOPE:].reshape(KV_LORA, -1)], axis=1)
    q_wb3 = q_w_b.reshape(Q_LORA, N_HEADS, QK_NOPE + QK_ROPE)
    q_wb = jnp.concatenate([q_wb3[:, :, :QK_NOPE].reshape(Q_LORA, -1),
                            q_wb3[:, :, QK_NOPE:].reshape(Q_LORA, -1)], axis=1)
    return w_a.astype(BF16), kv_wb.astype(BF16), q_wb.astype(BF16)


def _attn_kernel(q_ref, kn_ref, kr_ref, v_ref, o_ref, s_ref, m_ref, acc_ref):
    tq = ATTN_TQ
    half = tq // 2
    n_q = q_ref.shape[0] // tq
    blocks = []
    for qi in range(n_q):
        k_end = (qi + 1) * tq
        for k0 in range(0, k_end, ATTN_TK):
            nk = min(ATTN_TK, k_end - k0)
            blocks.append((qi, k0, nk, k0 + nk == k_end))
    nt = (((1,), (1,)), ((), ()))
    ones = jnp.ones((ATTN_TK, LANES), BF16)

    def pieces(nkeys, diag):
        return ((0, half, nkeys - half), (half, tq, nkeys)) if diag else ((0, tq, nkeys),)

    def scores(n):
        qi, k0, nkeys, diag = blocks[n]
        k = jnp.concatenate([kn_ref[k0:k0 + nkeys, :], kr_ref[k0:k0 + nkeys, :]], axis=1)
        for r0, r1, nk in pieces(nkeys, diag):
            s_ref[n % 2, r0:r1, :nk] = lax.dot_general(
                q_ref[qi * tq + r0:qi * tq + r1, :], k[:nk], nt, preferred_element_type=F32)

    def consume(n):
        qi, k0, nkeys, diag = blocks[n]
        buf = qi % 2
        for r0, r1, nk in pieces(nkeys, diag):
            s = s_ref[n % 2, r0:r1, :nk]
            if diag:
                sq = s[:, nk - half:]
                keep = (lax.broadcasted_iota(jnp.int32, sq.shape, 1)
                        <= lax.broadcasted_iota(jnp.int32, sq.shape, 0))
                sq = jnp.where(keep, sq, NEG_INF)
                s = sq if nk == half else jnp.concatenate([s[:, :nk - half], sq], axis=1)
            m_new = jnp.max(s, axis=-1, keepdims=True)
            if k0 == 0:
                m_new = jnp.broadcast_to(m_new, (r1 - r0, LANES))
            else:
                m_old = m_ref[buf, r0:r1, :]
                m_new = jnp.maximum(m_old, m_new)
            p = jnp.exp2((s - jnp.concatenate([m_new] * (nk // LANES), axis=1)).astype(BF16))
            v_ext = jnp.concatenate([v_ref[k0:k0 + nk, :], ones[:nk]], axis=1)
            acc = jnp.dot(p, v_ext, preferred_element_type=F32)
            if k0 > 0:
                alpha = jnp.exp2(m_old - m_new)
                acc = jnp.concatenate([alpha, alpha], axis=1) * acc_ref[buf, r0:r1, :] + acc
            if diag:
                o_ref[qi * tq + r0:qi * tq + r1, :] = (
                    acc[:, :V_HEAD] / acc[:, V_HEAD:]).astype(o_ref.dtype)
            else:
                acc_ref[buf, r0:r1, :] = acc
                m_ref[buf, r0:r1, :] = m_new

    scores(0)
    for n in range(len(blocks)):
        if n + 1 < len(blocks):
            scores(n + 1)
        consume(n)


def _attention(q, kn, kr, v):
    nb, nh, seq, _ = q.shape
    assert seq % ATTN_TQ == 0
    return pl.pallas_call(
        _attn_kernel,
        grid=(nb, nh),
        in_specs=[pl.BlockSpec((None, None, seq, 2 * LANES), lambda b, h: (b, h, 0, 0)),
                  pl.BlockSpec((None, None, seq, QK_NOPE), lambda b, h: (b, h, 0, 0)),
                  pl.BlockSpec((None, seq, LANES), lambda b, h: (b, 0, 0)),
                  pl.BlockSpec((None, None, seq, V_HEAD), lambda b, h: (b, h, 0, 0))],
        out_specs=pl.BlockSpec((None, seq, V_HEAD), lambda b, h: (b, 0, h)),
        out_shape=jax.ShapeDtypeStruct((nb, seq, nh * V_HEAD), BF16),
        scratch_shapes=[pltpu.VMEM((2, ATTN_TQ, ATTN_TK), F32),
                        pltpu.VMEM((2, ATTN_TQ, LANES), F32),
                        pltpu.VMEM((2, ATTN_TQ, V_HEAD + LANES), F32)],
        compiler_params=pltpu.CompilerParams(
            dimension_semantics=("parallel", "parallel"), vmem_limit_bytes=VMEM_LIMIT_BYTES),
        name="attention",
    )(q, kn, kr, v)


def _rope_inv_freq():
    inv_freq = ROPE_THETA ** (-jnp.arange(ROPE_HALF, dtype=F32) / ROPE_HALF)
    group = jnp.concatenate([-inv_freq, inv_freq])
    return jnp.tile(group, LANES // QK_ROPE).reshape(1, LANES)


def _row(v):
    return v.reshape(1, -1).astype(F32)


def kernel(x, positions, ln_mix_g, ln_mix_b, ln_ffn_g, ln_ffn_b, w_ff1, w_ff2, ssm_lam_re, ssm_lam_im, ssm_log_dt, ssm_b_re, ssm_b_im, ssm_c_re, ssm_c_im, ssm_d, ssm_w_glu, ssm_w_out, kv_w_a, kv_norm_g, kv_w_b, q_w_a, q_norm_g, q_w_b, attn_w_o):
    nb, seq, _ = x.shape
    m = nb * seq
    pos = positions.astype(F32).reshape(m, 1)
    inv_freq = _rope_inv_freq()
    w1_all = w_ff1.astype(BF16)
    w2_all = w_ff2.astype(BF16)

    assert N_A_LAYERS == 1 and DEPTH == 2
    h = x.reshape(m, D_MODEL)
    ln1 = [(_row(ln_mix_g[k]), _row(ln_mix_b[k])) for k in range(DEPTH)]
    ln2 = [(_row(ln_ffn_g[k]), _row(ln_ffn_b[k])) for k in range(DEPTH)]

    s5_params = _s5_params(ssm_lam_re[0], ssm_lam_im[0], ssm_log_dt[0],
                           ssm_b_re[0], ssm_b_im[0], ssm_c_re[0], ssm_c_im[0])
    y = _s5_scan(h.reshape(nb, seq, D_MODEL), s5_params, _row(ssm_d[0]))
    w_a, kv_wb, q_wb = _mla_weights(kv_w_a, kv_w_b, q_w_a[0], q_w_b[0])
    h, q, kn, kr, v = _s5_tail_mlp_proj(
        y.reshape(m, D_MODEL), h, pos,
        (ssm_w_glu[0].astype(BF16), ssm_w_out[0].astype(BF16)) + ln1[0], w1_all, w2_all, 0, ln2[0],
        (inv_freq, w_a, _row(kv_norm_g), kv_wb, _row(q_norm_g[0]), q_wb), nb, seq)
    o = _attention(q, kn, kr, v)
    h = _row_stage(_attn_out_mlp_kernel, "attn_out_mlp", (o.reshape(m, N_HEADS * V_HEAD), h),
                   (attn_w_o[0].astype(BF16),) + ln1[1], w1_all, w2_all, 1, ln2[1], tm=2 * ROW_TILE)
    return h.reshape(nb, seq, D_MODEL)
```

```python
import math

import jax
import numpy as np
import jax.numpy as jnp
from jax import lax
from jax.experimental import pallas as pl
from jax.experimental.pallas import tpu as pltpu

F32 = jnp.float32
BF16 = jnp.bfloat16

D_MODEL = 1024
DEPTH = 2
SSM_GROUP = 16
N_GROUPS = D_MODEL // SSM_GROUP
SSM_STATE = 64
N_HEADS = 8
QK_NOPE = 128
QK_ROPE = 64
V_HEAD = 128
Q_LORA = 384
KV_LORA = 256
ROPE_THETA = 10000.0
SM_SCALE = (QK_NOPE + QK_ROPE) ** -0.5
Q_SCALE = SM_SCALE * math.log2(math.e)
NEG_INF = -1e30
D_FF = 4 * D_MODEL
N_A_LAYERS = DEPTH // 2
DN_ALPHA = (2 * DEPTH) ** 0.25
LN_EPS = 1e-5
RMS_EPS = 1e-6

LANES = 128
SUBLANES = 8
MXU_DIM = 256
VMEM_LIMIT_BYTES = 56 * 1024 * 1024
VMEM_LIMIT_FUSED_BYTES = 58 * 1024 * 1024

S5_L = 8
S5_T = 256
S5_PAIRS = N_GROUPS // 2
S5_N_SLABS = D_MODEL // LANES
S5_SUPER = 4 * LANES
S5_PITCH = S5_T + SUBLANES

ROW_TILE = 512
ATTN_TQ = 512
ATTN_TK = 1024


def _layer_norm(x, g, b):
    mu = jnp.mean(x, axis=-1, keepdims=True)
    xc = x - mu
    var = jnp.mean(xc * xc, axis=-1, keepdims=True)
    return xc * lax.rsqrt(var + LN_EPS) * g + b


def _rms_norm(x, g):
    return x * lax.rsqrt(jnp.mean(x * x, axis=-1, keepdims=True) + RMS_EPS) * g


def _gelu_tanh(x):
    c = math.sqrt(2.0 / math.pi)
    half_x = 0.5 * x
    return half_x + half_x * jnp.tanh(x * (c + (0.044715 * c) * (x * x)))


def _const_spec(shape):
    nd = len(shape)
    return pl.BlockSpec(shape, lambda *_: (0,) * nd, pipeline_mode=pl.Buffered(1))


def _s5_kernel(x_ref, perm_ref, permt_ref, wi_ref, wt_ref, wo_ref, a8r_ref, a8i_ref, d_ref, o_ref,
               xs_ref, hc_ref):
    nb, t_len, _ = x_ref.shape
    n_chunks = t_len // S5_L
    halves = S5_L * LANES // S5_SUPER

    @pl.when(pl.program_id(0) == 0)
    def _():
        hc_ref[...] = jnp.zeros_like(hc_ref)

    for b in range(nb):
        x_b = x_ref[b]
        for l in range(S5_N_SLABS):
            xs_ref[l, pl.ds(b * S5_PITCH, t_len), :] = x_b[:, l * LANES:(l + 1) * LANES]

    def step_rows(c, j):
        return pl.ds(S5_L * c + j, nb, stride=S5_PITCH)

    def gather(l):
        z_half = []
        for a in range(halves):
            lhs = jnp.concatenate(
                [jnp.concatenate([xs_ref[l, step_rows(c, 4 * a + jj), :] for c in range(n_chunks)],
                                 axis=0) for jj in range(4)], axis=1).astype(BF16)
            z_half.append(jnp.dot(lhs, perm_ref[...], preferred_element_type=F32).astype(BF16))
        return z_half

    def mix(l, z_half):
        y_pair = []
        for gh in range(4):
            q = 4 * l + gh
            z_q = jnp.concatenate([zh[:, gh * LANES:(gh + 1) * LANES] for zh in z_half], axis=1)
            w_in = jnp.concatenate([wi_ref[j, q] for j in range(S5_L)], axis=0)
            w_y = jnp.concatenate([wt_ref[j, q] for j in range(S5_L)] + [wo_ref[0, q], wo_ref[1, q]],
                                  axis=0)
            s_q = jnp.dot(z_q, w_in, preferred_element_type=F32)
            a_re, a_im = a8r_ref[q], a8i_ref[q]
            h_re, h_im = hc_ref[2 * q], hc_ref[2 * q + 1]
            ent_re, ent_im = [], []
            for c in range(n_chunks):
                ent_re.append(h_re)
                ent_im.append(h_im)
                rows_c = slice(c * nb, (c + 1) * nb)
                n_re = a_re * h_re - a_im * h_im + s_q[rows_c, :LANES]
                n_im = a_re * h_im + a_im * h_re + s_q[rows_c, LANES:]
                h_re, h_im = n_re, n_im
            hc_ref[2 * q] = h_re
            hc_ref[2 * q + 1] = h_im
            ent = jnp.concatenate([jnp.concatenate(ent_re, axis=0),
                                   jnp.concatenate(ent_im, axis=0)], axis=1).astype(BF16)
            y_pair.append(jnp.dot(jnp.concatenate([z_q, ent], axis=1), w_y,
                                  preferred_element_type=F32))
        return y_pair

    def emit(l, y_pair):
        d_l = d_ref[:, l * LANES:(l + 1) * LANES]
        for a in range(halves):
            y_l = jnp.concatenate([y[:, a * LANES:(a + 1) * LANES] for y in y_pair], axis=1)
            out = jnp.dot(y_l.astype(BF16), permt_ref[...], preferred_element_type=F32)
            for jj in range(4):
                for c in range(n_chunks):
                    rows = step_rows(c, 4 * a + jj)
                    xs_ref[l, rows, :] = (out[c * nb:(c + 1) * nb, jj * LANES:(jj + 1) * LANES]
                                          + d_l * xs_ref[l, rows, :])
        for b in range(nb):
            o_ref[b, :, l * LANES:(l + 1) * LANES] = _gelu_tanh(
                xs_ref[l, pl.ds(b * S5_PITCH, t_len), :]).astype(o_ref.dtype)

    z_next = gather(0)
    for l in range(S5_N_SLABS):
        y_pair = mix(l, z_next)
        if l + 1 < S5_N_SLABS:
            z_next = gather(l + 1)
        emit(l, y_pair)


def _s5_scan(x, params, d_skip, t_len=S5_T):
    nb, seq, _ = x.shape
    assert nb == SUBLANES and seq % t_len == 0 and t_len % S5_L == 0
    return pl.pallas_call(
        _s5_kernel,
        grid=(seq // t_len,),
        in_specs=[pl.BlockSpec((nb, t_len, D_MODEL), lambda i: (0, i, 0))]
        + [_const_spec(p.shape) for p in params] + [_const_spec(d_skip.shape)],
        out_specs=pl.BlockSpec((nb, t_len, D_MODEL), lambda i: (0, i, 0)),
        out_shape=jax.ShapeDtypeStruct(x.shape, BF16),
        scratch_shapes=[
            pltpu.VMEM((S5_N_SLABS, nb * S5_PITCH, LANES), F32),
            pltpu.VMEM((N_GROUPS, SUBLANES, LANES), F32),
        ],
        compiler_params=pltpu.CompilerParams(
            dimension_semantics=("arbitrary",), vmem_limit_bytes=VMEM_LIMIT_BYTES),
        name="s5_scan",
    )(x, *params, d_skip)


def _s5_params(lam_re, lam_im, log_dt, b_re, b_im, c_re, c_im):
    dt = jnp.exp(log_dt)[:, None]
    mag = jnp.exp(lam_re * dt)
    a_re = mag * jnp.cos(lam_im * dt)
    a_im = mag * jnp.sin(lam_im * dt)
    inv_den = 1.0 / (lam_re * lam_re + lam_im * lam_im)
    coef_re = ((a_re - 1.0) * lam_re + a_im * lam_im) * inv_den
    coef_im = (a_im * lam_re - (a_re - 1.0) * lam_im) * inv_den
    bb_re = coef_re[..., None] * b_re - coef_im[..., None] * b_im
    bb_im = coef_re[..., None] * b_im + coef_im[..., None] * b_re
    pw_re, pw_im = [jnp.ones_like(a_re)], [jnp.zeros_like(a_im)]
    for _ in range(S5_L):
        pw_re, pw_im = (pw_re + [pw_re[-1] * a_re - pw_im[-1] * a_im],
                        pw_im + [pw_re[-1] * a_im + pw_im[-1] * a_re])
    pw_re, pw_im = jnp.stack(pw_re), jnp.stack(pw_im)
    ca_re = c_re[None] * pw_re[:, :, None, :] - c_im[None] * pw_im[:, :, None, :]
    ca_im = c_re[None] * pw_im[:, :, None, :] + c_im[None] * pw_re[:, :, None, :]
    kern = (jnp.einsum('kgop,gpi->kgio', ca_re[:S5_L], bb_re)
            - jnp.einsum('kgop,gpi->kgio', ca_im[:S5_L], bb_im))
    k_c = jnp.transpose(kern, (1, 2, 0, 3)).astype(BF16).reshape(
        S5_PAIRS, 2, SSM_GROUP, LANES)
    rev = S5_L - 1 - np.arange(S5_L)
    rev_re, rev_im = pw_re[rev][..., None], pw_im[rev][..., None]
    ab = jnp.stack([rev_re * bb_re[None] - rev_im * bb_im[None],
                    rev_re * bb_im[None] + rev_im * bb_re[None]])
    a_c = jnp.transpose(ab, (1, 2, 4, 0, 3)).astype(BF16).reshape(
        S5_L, S5_PAIRS, 2, SSM_GROUP, LANES)
    co = jnp.stack([ca_re[1:], -ca_im[1:]])
    c_c = jnp.transpose(co, (0, 2, 4, 1, 3)).astype(BF16).reshape(
        2, S5_PAIRS, 2, SSM_STATE, LANES)
    wt, wi, wo = _s5_weights(k_c, a_c, c_c)

    src = np.arange(S5_SUPER).reshape(4, 4, 2 * SSM_GROUP)
    dst = np.transpose(src, (1, 0, 2)).reshape(-1)
    perm = jnp.asarray(dst[None, :] == np.arange(S5_SUPER)[:, None], BF16)

    def bcast(a):
        return jnp.broadcast_to(a.reshape(S5_PAIRS, 1, LANES), (S5_PAIRS, SUBLANES, LANES))

    return perm, perm.T, wi, wt, wo, bcast(pw_re[S5_L]), bcast(pw_im[S5_L])


def _s5_placements():
    c, p, n = SSM_GROUP, SSM_STATE, S5_L
    shift = np.zeros((n, 2, n, c, n, 2, c), np.float32)
    place_y = np.zeros((2, n, c, n, 2, c), np.float32)
    place_s = np.zeros((2, 2, p, 2, 2, p), np.float32)
    for g in range(2):
        for k in range(n):
            place_y[g, k, :, k, g, :] = np.eye(c)
            for j in range(k + 1):
                shift[j, g, k - j, :, k, g, :] = np.eye(c)
        for part in range(2):
            place_s[g, part, :, part, g, :] = np.eye(p)
    as_bf16 = lambda m, lead: jnp.asarray(m.reshape(lead + (LANES, 2 * LANES)), BF16)
    return as_bf16(shift, (n, 2)), as_bf16(place_s, (2,)), as_bf16(place_y, (2,))


def _s5_weights_kernel(k_ref, a_ref, c_ref, shift_ref, place_s_ref, place_y_ref, wt_ref, wi_ref, wo_ref):
    n_pairs = k_ref.shape[0]
    for g in range(2):
        k_g = k_ref[:, g].reshape(n_pairs * SSM_GROUP, LANES)
        for j in range(S5_L):
            wt_ref[j, :, g] = jnp.dot(k_g, shift_ref[j, g], preferred_element_type=F32).astype(
                wt_ref.dtype).reshape(n_pairs, SSM_GROUP, 2 * LANES)
            a_jg = a_ref[j, :, g].reshape(n_pairs * SSM_GROUP, LANES)
            wi_ref[j, :, g] = jnp.dot(a_jg, place_s_ref[g], preferred_element_type=F32).astype(
                wi_ref.dtype).reshape(n_pairs, SSM_GROUP, 2 * LANES)
        for part in range(2):
            c_pg = c_ref[part, :, g].reshape(n_pairs * SSM_STATE, LANES)
            wo_ref[part, :, g] = jnp.dot(c_pg, place_y_ref[g], preferred_element_type=F32).astype(
                wo_ref.dtype).reshape(n_pairs, SSM_STATE, 2 * LANES)


def _s5_weights(k_c, a_c, c_c):
    wide = 2 * LANES
    wt, wi, wo = pl.pallas_call(
        _s5_weights_kernel,
        out_shape=[jax.ShapeDtypeStruct((S5_L, S5_PAIRS, 2, SSM_GROUP, wide), BF16),
                   jax.ShapeDtypeStruct((S5_L, S5_PAIRS, 2, SSM_GROUP, wide), BF16),
                   jax.ShapeDtypeStruct((2, S5_PAIRS, 2, SSM_STATE, wide), BF16)],
        compiler_params=pltpu.CompilerParams(vmem_limit_bytes=VMEM_LIMIT_BYTES),
        name="s5_weights",
    )(k_c, a_c, c_c, *_s5_placements())
    return (wt.reshape(S5_L, S5_PAIRS, 2 * SSM_GROUP, wide), wi.reshape(S5_L, S5_PAIRS, 2 * SSM_GROUP, wide),
            wo.reshape(2, S5_PAIRS, 2 * SSM_STATE, wide))


ROW_SUBTILES = 2
MLP_CHUNKS = D_FF // D_MODEL


def _skewed(n_stages, stage):
    for step in range(n_stages + ROW_SUBTILES - 1):
        for r in range(ROW_SUBTILES):
            k = step - r
            if 0 <= k < n_stages:
                stage(r, k)


def _sub_rows(ref, r):
    sub = ref.shape[0] // ROW_SUBTILES
    return slice(r * sub, (r + 1) * sub)


MLP_STAGES = 2 * MLP_CHUNKS + 2


def _mlp_stages(d, k, rows, g1_ref, b1_ref, w1_ref, w2_ref, g2_ref, b2_ref, o_ref):
    if k == 0:
        h = _layer_norm(d.pop("pre"), g1_ref[...], b1_ref[...])
        d["hb"] = h.astype(BF16)
        d["acc"] = DN_ALPHA * h
    elif k <= 2 * MLP_CHUNKS:
        c = (k - 1) // 2
        cols = slice(c * D_MODEL, (c + 1) * D_MODEL)
        if (k - 1) % 2 == 0:
            u = jnp.maximum(jnp.dot(d["hb"], w1_ref[:, cols], preferred_element_type=F32), 0.0)
            d["u"] = (u * u).astype(BF16)
        else:
            d["acc"] = d["acc"] + jnp.dot(d.pop("u"), w2_ref[cols, :], preferred_element_type=F32)
    else:
        d["out"] = _layer_norm(d.pop("acc"), g2_ref[...], b2_ref[...])
        o_ref[rows, :] = d["out"]


def _s5_tail_mlp_proj_kernel(y_ref, x_ref, pos_ref, wg_ref, wo_ref, g1_ref, b1_ref, w1_ref, w2_ref,
                             g2_ref, b2_ref, invf_ref, wa_ref, kvg_ref, kvwb_ref, qg_ref, qwb_ref,
                             o_ref, q_ref, kn_ref, kr_ref, v_ref):
    state = [dict() for _ in range(ROW_SUBTILES)]
    n_mlp = MLP_STAGES

    def stage(r, k):
        d, rows = state[r], _sub_rows(o_ref, r)
        if k == 0:
            d["z"] = jnp.dot(y_ref[rows, :], wg_ref[...], preferred_element_type=F32)
        elif k == 1:
            z = d.pop("z")
            m = (z[:, :D_MODEL] * jax.nn.sigmoid(z[:, D_MODEL:])).astype(BF16)
            d["pre"] = DN_ALPHA * x_ref[rows, :] + jnp.dot(m, wo_ref[...],
                                                           preferred_element_type=F32)
        elif k < 2 + n_mlp:
            _mlp_stages(d, k - 2, rows, g1_ref, b1_ref, w1_ref, w2_ref, g2_ref, b2_ref, o_ref)
        else:
            _proj_stages(d, k - 2 - n_mlp, rows, pos_ref, invf_ref, wa_ref, kvg_ref, kvwb_ref,
                         qg_ref, qwb_ref, q_ref, kn_ref, kr_ref, v_ref)

    _skewed(2 + n_mlp + PROJ_STAGES, stage)


def _attn_out_mlp_kernel(a_ref, r_ref, wo_ref, g1_ref, b1_ref, w1_ref, w2_ref, g2_ref, b2_ref, o_ref):
    state = [dict() for _ in range(ROW_SUBTILES)]

    def stage(r, k):
        d, rows = state[r], _sub_rows(o_ref, r)
        if k == 0:
            d["pre"] = DN_ALPHA * r_ref[rows, :] + jnp.dot(a_ref[rows, :], wo_ref[...],
                                                           preferred_element_type=F32)
        else:
            _mlp_stages(d, k - 1, rows, g1_ref, b1_ref, w1_ref, w2_ref, g2_ref, b2_ref, o_ref)

    _skewed(MLP_STAGES + 1, stage)


def _row_stage(body, name, row_inputs, const_inputs, w1_all, w2_all, layer, ln2, tm=ROW_TILE):
    m = row_inputs[0].shape[0]
    row = lambda i: (i, 0)
    layer_spec = lambda w: pl.BlockSpec((None,) + w.shape[1:], lambda i: (layer, 0, 0),
                                        pipeline_mode=pl.Buffered(1))
    return pl.pallas_call(
        body,
        grid=(m // tm,),
        in_specs=([pl.BlockSpec((tm, a.shape[1]), row) for a in row_inputs]
                  + [_const_spec(c.shape) for c in const_inputs]
                  + [layer_spec(w1_all), layer_spec(w2_all)]
                  + [_const_spec(c.shape) for c in ln2]),
        out_specs=pl.BlockSpec((tm, D_MODEL), row),
        out_shape=jax.ShapeDtypeStruct((m, D_MODEL), F32),
        compiler_params=pltpu.CompilerParams(
            dimension_semantics=("parallel",), vmem_limit_bytes=VMEM_LIMIT_BYTES),
        name=name,
    )(*row_inputs, *const_inputs, w1_all, w2_all, *ln2)


ROPE_HALF = QK_ROPE // 2
PROJ_A_COLS = 3 * MXU_DIM
Q_ROPE_TILES = N_HEADS * QK_ROPE // LANES


PROJ_STAGES = 4


def _proj_stages(d, k, rows, pos_ref, invf_ref, wa_ref, kvg_ref, kvwb_ref, qg_ref, qwb_ref,
                 q_ref, kn_ref, kr_ref, v_ref):
    c_q0 = KV_LORA
    k_rope0 = KV_LORA + Q_LORA

    def rope(x):
        swapped = jnp.where(d["first"], pltpu.roll(x, LANES - ROPE_HALF, axis=1),
                            pltpu.roll(x, ROPE_HALF, axis=1))
        return x * d["cos"] + swapped * d["sin"]

    if k == 0:
        d["a"] = jnp.dot(d.pop("out").astype(BF16), wa_ref[...], preferred_element_type=F32)
        ang = pos_ref[rows, :] * invf_ref[...]
        lane = lax.broadcasted_iota(jnp.int32, ang.shape, 1)
        d["first"] = (lane & ROPE_HALF) == 0
        d["low"] = lane < QK_ROPE
        d["cos"] = jnp.cos(ang)
        d["sin"] = jnp.sin(ang)
    elif k == 1:
        a = d.pop("a")
        kr_ref[rows, :] = rope(a[:, k_rope0:k_rope0 + LANES]).astype(kr_ref.dtype)
        c_kv = _rms_norm(a[:, :KV_LORA], kvg_ref[...]).astype(BF16)
        d["c_q"] = _rms_norm(a[:, c_q0:c_q0 + Q_LORA], qg_ref[...]).astype(BF16)
        d["kv"] = jnp.dot(c_kv, kvwb_ref[...], preferred_element_type=F32)
    elif k == 2:
        kv = d.pop("kv")
        for hd in range(N_HEADS):
            kn_ref[hd, rows, :] = kv[:, hd * QK_NOPE:(hd + 1) * QK_NOPE].astype(kn_ref.dtype)
            v0 = N_HEADS * QK_NOPE + hd * V_HEAD
            v_ref[hd, rows, :] = kv[:, v0:v0 + V_HEAD].astype(v_ref.dtype)
        d["q"] = jnp.dot(d.pop("c_q"), qwb_ref[...], preferred_element_type=F32) * Q_SCALE
    else:
        q = d.pop("q")
        for hd in range(N_HEADS):
            q_ref[hd, rows, :LANES] = q[:, hd * LANES:(hd + 1) * LANES].astype(q_ref.dtype)
        for p in range(Q_ROPE_TILES):
            c0 = N_HEADS * QK_NOPE + p * LANES
            y = rope(q[:, c0:c0 + LANES])
            for hd, y_hd in ((2 * p, y), (2 * p + 1, pltpu.roll(y, QK_ROPE, axis=1))):
                q_ref[hd, rows, LANES:] = jnp.where(d["low"], y_hd, 0.0).astype(q_ref.dtype)


def _s5_tail_mlp_proj(y, x, pos, tail_consts, w1_all, w2_all, layer, ln2, proj_consts, nb, seq,
                      tm=ROW_TILE):
    nt = seq // tm
    row = lambda b, i: (b * nt + i, 0)
    head = lambda b, i: (b, 0, i, 0)
    layer_spec = lambda w: pl.BlockSpec((None,) + w.shape[1:], lambda b, i: (layer, 0, 0),
                                        pipeline_mode=pl.Buffered(1))
    consts = lambda cs: [_const_spec(c.shape) for c in cs]
    return pl.pallas_call(
        _s5_tail_mlp_proj_kernel,
        grid=(nb, nt),
        in_specs=([pl.BlockSpec((tm, D_MODEL), row), pl.BlockSpec((tm, D_MODEL), row),
                   pl.BlockSpec((tm, 1), row)]
                  + consts(tail_consts) + [layer_spec(w1_all), layer_spec(w2_all)]
                  + consts(ln2) + consts(proj_consts)),
        out_specs=[pl.BlockSpec((tm, D_MODEL), row),
                   pl.BlockSpec((None, N_HEADS, tm, 2 * LANES), head),
                   pl.BlockSpec((None, N_HEADS, tm, QK_NOPE), head),
                   pl.BlockSpec((None, tm, LANES), lambda b, i: (b, i, 0)),
                   pl.BlockSpec((None, N_HEADS, tm, V_HEAD), head)],
        out_shape=[jax.ShapeDtypeStruct((nb * seq, D_MODEL), F32),
                   jax.ShapeDtypeStruct((nb, N_HEADS, seq, 2 * LANES), BF16),
                   jax.ShapeDtypeStruct((nb, N_HEADS, seq, QK_NOPE), BF16),
                   jax.ShapeDtypeStruct((nb, seq, LANES), BF16),
                   jax.ShapeDtypeStruct((nb, N_HEADS, seq, V_HEAD), BF16)],
        compiler_params=pltpu.CompilerParams(
            dimension_semantics=("parallel", "parallel"), vmem_limit_bytes=VMEM_LIMIT_FUSED_BYTES),
        name="s5_tail_mlp_proj",
    )(y, x, pos, *tail_consts, w1_all, w2_all, *ln2, *proj_consts)


def _mla_weights(kv_w_a, kv_w_b, q_w_a, q_w_b):
    pad = jnp.zeros((D_MODEL, PROJ_A_COLS - KV_LORA - Q_LORA - QK_ROPE), F32)
    w_a = jnp.concatenate([kv_w_a[:, :KV_LORA], q_w_a, kv_w_a[:, KV_LORA:], pad], axis=1)
    kv_wb3 = kv_w_b.reshape(KV_LORA, N_HEADS, QK_NOPE + V_HEAD)
    kv_wb = jnp.concatenate([kv_wb3[:, :, :QK_NOPE].reshape(KV_LORA, -1),
                             kv_wb3[:, :, QK_NOPE:].reshape(KV_LORA, -1)], axis=1)
    q_wb3 = q_w_b.reshape(Q_LORA, N_HEADS, QK_NOPE + QK_ROPE)
    q_wb = jnp.concatenate([q_wb3[:, :, :QK_NOPE].reshape(Q_LORA, -1),
                            q_wb3[:, :, QK_NOPE:].reshape(Q_LORA, -1)], axis=1)
    return w_a.astype(BF16), kv_wb.astype(BF16), q_wb.astype(BF16)


def _attn_kernel(q_ref, kn_ref, kr_ref, v_ref, o_ref, s_ref, m_ref, acc_ref):
    tq = ATTN_TQ
    half = tq // 2
    n_q = q_ref.shape[0] // tq
    blocks = []
    for qi in range(n_q):
        k_end = (qi + 1) * tq
        for k0 in range(0, k_end, ATTN_TK):
            nk = min(ATTN_TK, k_end - k0)
            blocks.append((qi, k0, nk, k0 + nk == k_end))
    nt = (((1,), (1,)), ((), ()))
    ones = jnp.ones((ATTN_TK, LANES), BF16)

    def pieces(nkeys, diag):
        return ((0, half, nkeys - half), (half, tq, nkeys)) if diag else ((0, tq, nkeys),)

    def scores(n):
        qi, k0, nkeys, diag = blocks[n]
        k = jnp.concatenate([kn_ref[k0:k0 + nkeys, :], kr_ref[k0:k0 + nkeys, :]], axis=1)
        for r0, r1, nk in pieces(nkeys, diag):
            s_ref[n % 2, r0:r1, :nk] = lax.dot_general(
                q_ref[qi * tq + r0:qi * tq + r1, :], k[:nk], nt, preferred_element_type=F32)

    def consume(n):
        qi, k0, nkeys, diag = blocks[n]
        buf = qi % 2
        for r0, r1, nk in pieces(nkeys, diag):
            s = s_ref[n % 2, r0:r1, :nk]
            if diag:
                sq = s[:, nk - half:]
                keep = (lax.broadcasted_iota(jnp.int32, sq.shape, 1)
                        <= lax.broadcasted_iota(jnp.int32, sq.shape, 0))
                sq = jnp.where(keep, sq, NEG_INF)
                s = sq if nk == half else jnp.concatenate([s[:, :nk - half], sq], axis=1)
            m_new = jnp.max(s, axis=-1, keepdims=True)
            if k0 == 0:
                m_new = jnp.broadcast_to(m_new, (r1 - r0, LANES))
            else:
                m_old = m_ref[buf, r0:r1, :]
                m_new = jnp.maximum(m_old, m_new)
            p = jnp.exp2((s - jnp.concatenate([m_new] * (nk // LANES), axis=1)).astype(BF16))
            v_ext = jnp.concatenate([v_ref[k0:k0 + nk, :], ones[:nk]], axis=1)
            acc = jnp.dot(p, v_ext, preferred_element_type=F32)
            if k0 > 0:
                alpha = jnp.exp2(m_old - m_new)
                acc = jnp.concatenate([alpha, alpha], axis=1) * acc_ref[buf, r0:r1, :] + acc
            if diag:
                o_ref[qi * tq + r0:qi * tq + r1, :] = (
                    acc[:, :V_HEAD] / acc[:, V_HEAD:]).astype(o_ref.dtype)
            else:
                acc_ref[buf, r0:r1, :] = acc
                m_ref[buf, r0:r1, :] = m_new

    scores(0)
    for n in range(len(blocks)):
        if n + 1 < len(blocks):
            scores(n + 1)
        consume(n)


def _attention(q, kn, kr, v):
    nb, nh, seq, _ = q.shape
    assert seq % ATTN_TQ == 0
    return pl.pallas_call(
        _attn_kernel,
        grid=(nb, nh),
        in_specs=[pl.BlockSpec((None, None, seq, 2 * LANES), lambda b, h: (b, h, 0, 0)),
                  pl.BlockSpec((None, None, seq, QK_NOPE), lambda b, h: (b, h, 0, 0)),
                  pl.BlockSpec((None, seq, LANES), lambda b, h: (b, 0, 0)),
                  pl.BlockSpec((None, None, seq, V_HEAD), lambda b, h: (b, h, 0, 0))],
        out_specs=pl.BlockSpec((None, seq, V_HEAD), lambda b, h: (b, 0, h)),
        out_shape=jax.ShapeDtypeStruct((nb, seq, nh * V_HEAD), BF16),
        scratch_shapes=[pltpu.VMEM((2, ATTN_TQ, ATTN_TK), F32),
                        pltpu.VMEM((2, ATTN_TQ, LANES), F32),
                        pltpu.VMEM((2, ATTN_TQ, V_HEAD + LANES), F32)],
        compiler_params=pltpu.CompilerParams(
            dimension_semantics=("parallel", "parallel"), vmem_limit_bytes=VMEM_LIMIT_BYTES),
        name="attention",
    )(q, kn, kr, v)


def _rope_inv_freq():
    inv_freq = ROPE_THETA ** (-jnp.arange(ROPE_HALF, dtype=F32) / ROPE_HALF)
    group = jnp.concatenate([-inv_freq, inv_freq])
    return jnp.tile(group, LANES // QK_ROPE).reshape(1, LANES)


def _row(v):
    return v.reshape(1, -1).astype(F32)


def kernel(x, positions, ln_mix_g, ln_mix_b, ln_ffn_g, ln_ffn_b, w_ff1, w_ff2, ssm_lam_re, ssm_lam_im, ssm_log_dt, ssm_b_re, ssm_b_im, ssm_c_re, ssm_c_im, ssm_d, ssm_w_glu, ssm_w_out, kv_w_a, kv_norm_g, kv_w_b, q_w_a, q_norm_g, q_w_b, attn_w_o):
    nb, seq, _ = x.shape
    m = nb * seq
    pos = positions.astype(F32).reshape(m, 1)
    inv_freq = _rope_inv_freq()
    w1_all = w_ff1.astype(BF16)
    w2_all = w_ff2.astype(BF16)

    assert N_A_LAYERS == 1 and DEPTH == 2
    h = x.reshape(m, D_MODEL)
    ln1 = [(_row(ln_mix_g[k]), _row(ln_mix_b[k])) for k in range(DEPTH)]
    ln2 = [(_row(ln_ffn_g[k]), _row(ln_ffn_b[k])) for k in range(DEPTH)]

    s5_params = _s5_params(ssm_lam_re[0], ssm_lam_im[0], ssm_log_dt[0],
                           ssm_b_re[0], ssm_b_im[0], ssm_c_re[0], ssm_c_im[0])
    y = _s5_scan(h.reshape(nb, seq, D_MODEL), s5_params, _row(ssm_d[0]))
    w_a, kv_wb, q_wb = _mla_weights(kv_w_a, kv_w_b, q_w_a[0], q_w_b[0])
    h, q, kn, kr, v = _s5_tail_mlp_proj(
        y.reshape(m, D_MODEL), h, pos,
        (ssm_w_glu[0].astype(BF16), ssm_w_out[0].astype(BF16)) + ln1[0], w1_all, w2_all, 0, ln2[0],
        (inv_freq, w_a, _row(kv_norm_g), kv_wb, _row(q_norm_g[0]), q_wb), nb, seq)
    o = _attention(q, kn, kr, v)
    h = _row_stage(_attn_out_mlp_kernel, "attn_out_mlp", (o.reshape(m, N_HEADS * V_HEAD), h),
                   (attn_w_o[0].astype(BF16),) + ln1[1], w1_all, w2_all, 1, ln2[1], tm=2 * ROW_TILE)
    return h.reshape(nb, seq, D_MODEL)
```
